```python
import math
import jax, jax.numpy as jnp
from jax import lax
import numpy as np

D_MODEL = 1024
BATCH = 8
SEQ = 2048
DEPTH = 4
DEC_BATCH = 32
DEC_SEQ = 4
PAST_LEN = 8192
PAGE_SIZE = 128

DK_A = 128
DV_A = 128
H_A = (D_MODEL // 2) // DV_A
DH_B = 64
DV_B = 2 * DH_B
H_B = (D_MODEL // 2) // DV_B
W_A = H_A * DV_A
W_B = H_B * DV_B
MIX_W = W_A + W_B
IN_SPLIT = (H_A * DK_A, H_A * DK_A, W_A, W_A, 2 * H_B * DH_B, 2 * H_B * DH_B, W_B)
IN_COLS = 2 * H_A * DK_A + 2 * W_A + 4 * H_B * DH_B + W_B
D_FF = 256 * ((8 * D_MODEL // 3 + 255) // 256)
CONV_W = 3
CHUNK = 64
Q_BLOCK = 128
ALPHA = (2 * DEPTH) ** 0.25
BETA = (8 * DEPTH) ** -0.25
LN_EPS = 1e-5
RMS_EPS = 1e-6

kernel_name = 'hymba_hgrn2_diffattn_convffn_deepnorm_step'


def _layernorm(x, g, b):
    xf = x.astype(jnp.float32)
    mu = jnp.mean(xf, axis=-1, keepdims=True)
    var = jnp.mean(jnp.square(xf - mu), axis=-1, keepdims=True)
    y = (xf - mu) * lax.rsqrt(var + LN_EPS) * g.astype(jnp.float32) + b.astype(jnp.float32)
    return y.astype(x.dtype)


def _rmsnorm(x, w):
    xf = x.astype(jnp.float32)
    return xf * lax.rsqrt(jnp.mean(jnp.square(xf), axis=-1, keepdims=True) + RMS_EPS) * w.astype(jnp.float32)


def _split_cols(h):
    idx = np.cumsum(np.array(IN_SPLIT))[:-1].tolist()
    return jnp.split(h, idx, axis=-1)


def _chunk_len(t):
    return CHUNK if t % CHUNK == 0 else t


def _lower_bounds(logits):
    c = jnp.cumsum(jax.nn.softmax(logits.astype(jnp.float32), axis=0), axis=0)
    return c - c[0:1]


def _alibi_slopes():
    return 2.0 ** (-8.0 * jnp.arange(1, H_B + 1, dtype=jnp.float32) / H_B)


def _hgrn2_chunkwise(q, logf, v, s0, chunk):
    B, T, H, K = q.shape
    V = v.shape[-1]
    nc = T // chunk

    def to_chunks(a):
        return jnp.moveaxis(a.astype(jnp.float32).reshape(B, nc, chunk, H, a.shape[-1]), 1, 0)

    causal = jnp.tril(jnp.ones((chunk, chunk), dtype=bool))[None, :, :, None, None]

    def step(S, inp):
        qc, gc, vc = inp
        b = jnp.cumsum(gc, axis=1)
        kc = -jnp.expm1(gc)
        o_inter = jnp.einsum('bthk,bhkv->bthv', qc * jnp.exp(b), S)
        diff = b[:, :, None] - b[:, None, :]
        decay = jnp.exp(jnp.where(causal, diff, -jnp.inf))
        a = jnp.einsum('bthk,bshk,btshk->bhts', qc, kc, decay)
        o = o_inter + jnp.einsum('bhts,bshv->bthv', a, vc)
        b_last = b[:, -1]
        S_new = jnp.exp(b_last)[..., None] * S + jnp.einsum(
            'bshk,bshv->bhkv', kc * jnp.exp(b_last[:, None] - b), vc)
        return S_new, o

    S, o = lax.scan(step, s0.astype(jnp.float32), (to_chunks(q), to_chunks(logf), to_chunks(v)))
    return jnp.moveaxis(o, 0, 1).reshape(B, T, H, V), S


def _diff_attention(q, k, v, q_pos, k_pos, lam):
    B, T = q.shape[0], q.shape[1]
    blk = Q_BLOCK if T % Q_BLOCK == 0 else T
    nb = T // blk
    slopes = _alibi_slopes()[None, :, None, None, None]
    qf = q.astype(jnp.float32) * (DH_B ** -0.5)
    kf = k.astype(jnp.float32)
    vf = v.astype(jnp.float32)
    q_blocks = jnp.moveaxis(qf.reshape(B, nb, blk, H_B, 2, DH_B), 1, 0)
    p_blocks = q_pos.reshape(nb, blk)

    def one_block(args):
        qb, qp = args
        s = jnp.einsum('bqhmd,bkhmd->bhmqk', qb, kf)
        dist = qp[:, None] - k_pos[None, :]
        s = jnp.where(dist >= 0, s - slopes * dist.astype(jnp.float32), -jnp.inf)
        p = jax.nn.softmax(s, axis=-1)
        pd = p[:, :, 0] - lam * p[:, :, 1]
        return jnp.einsum('bhqk,bkhv->bqhv', pd, vf)

    o = lax.map(one_block, (q_blocks, p_blocks))
    return jnp.moveaxis(o, 0, 1).reshape(B, T, H_B, DV_B)


def _layer(x, pos0, past_k, past_v, s0, conv0, lb, lam_init,
           w_in, hgrn_norm_w, lq1, lk1, lq2, lk2, diff_norm_w, w_o,
           ln1_g, ln1_b, w_up, conv_w, conv_b, w_down, ln2_g, ln2_b):
    B, T, _ = x.shape
    qa, fa, ia, ga, qb, kb, vb = _split_cols(x @ w_in)
    qa = (jax.nn.silu(qa.astype(jnp.float32)) * (DK_A ** -0.5)).reshape(B, T, H_A, DK_A)
    lbf = lb.astype(jnp.float32)
    logf = jnp.logaddexp(jnp.log(lbf), jnp.log1p(-lbf) + jax.nn.log_sigmoid(fa.astype(jnp.float32)))
    logf = logf.reshape(B, T, H_A, DK_A)
    oa, s_new = _hgrn2_chunkwise(qa, logf, ia.reshape(B, T, H_A, DV_A), s0, _chunk_len(T))
    oa = _rmsnorm(oa, hgrn_norm_w) * jax.nn.silu(ga.astype(jnp.float32).reshape(B, T, H_A, DV_A))
    kb = kb.reshape(B, T, H_B, 2 * DH_B)
    vb = vb.reshape(B, T, H_B, DV_B)
    keys = kb if past_k is None else jnp.concatenate([past_k.astype(kb.dtype), kb], axis=1)
    vals = vb if past_v is None else jnp.concatenate([past_v.astype(vb.dtype), vb], axis=1)
    lam = (jnp.exp(jnp.sum(lq1.astype(jnp.float32) * lk1.astype(jnp.float32)))
           - jnp.exp(jnp.sum(lq2.astype(jnp.float32) * lk2.astype(jnp.float32))) + lam_init)
    q_pos = pos0 + jnp.arange(T, dtype=jnp.int32)
    k_pos = jnp.arange(keys.shape[1], dtype=jnp.int32)
    ob = _diff_attention(qb.reshape(B, T, H_B, 2, DH_B), keys.reshape(B, -1, H_B, 2, DH_B),
                         vals, q_pos, k_pos, lam)
    ob = _rmsnorm(ob, diff_norm_w) * (1.0 - lam_init)
    mixed = jnp.concatenate([oa.reshape(B, T, W_A), ob.reshape(B, T, W_B)], axis=-1).astype(x.dtype) @ w_o
    x = _layernorm(ALPHA * x + mixed, ln1_g, ln1_b)
    a, g = jnp.split(x @ w_up, [D_FF], axis=-1)
    a_pad = jnp.concatenate([conv0.astype(a.dtype), a], axis=1)
    c = conv_b + sum(a_pad[:, j:j + T] * conv_w[j] for j in range(CONV_W))
    y = (jax.nn.silu(c) * g) @ w_down
    x = _layernorm(ALPHA * x + y, ln2_g, ln2_b)
    return x, kb, vb, s_new, a_pad[:, T:]


def setup_inputs(seed: int = 0) -> dict:
    key = jax.random.key(seed)
    ks = jax.random.split(key, 32)
    f32 = jnp.float32

    def nrm(k, shape, scale):
        return scale * jax.random.normal(k, shape, f32)

    n_pages = PAST_LEN // PAGE_SIZE
    n_used = DEC_BATCH * n_pages
    n_phys = n_used + max(1, n_used // 4)
    page_table = jax.random.permutation(ks[0], n_phys)[:n_used].reshape(DEC_BATCH, n_pages).astype(jnp.int32)
    return {
        'x_prompt': nrm(ks[1], (BATCH, SEQ, D_MODEL), 1.0),
        'x_sample': nrm(ks[2], (DEC_BATCH, DEC_SEQ, D_MODEL), 1.0),
        'cache_k': nrm(ks[3], (DEPTH, n_phys, PAGE_SIZE, H_B, 2 * DH_B), 1.0),
        'cache_v': nrm(ks[4], (DEPTH, n_phys, PAGE_SIZE, H_B, DV_B), 1.0),
        'state_hgrn': nrm(ks[5], (DEPTH, DEC_BATCH, H_A, DK_A, DV_A), 0.5),
        'state_ffn_conv': nrm(ks[6], (DEPTH, DEC_BATCH, CONV_W - 1, D_FF), 1.0),
        'page_table': page_table,
        'ln_in_g': 1.0 + nrm(ks[7], (D_MODEL,), 0.02),
        'ln_in_b': nrm(ks[8], (D_MODEL,), 0.02),
        'w_in': nrm(ks[9], (DEPTH, D_MODEL, IN_COLS), D_MODEL ** -0.5),
        'hgrn_lb_logits': nrm(ks[10], (DEPTH, H_A * DK_A), 1.0),
        'hgrn_norm_w': 1.0 + nrm(ks[11], (DEPTH, DV_A), 0.02),
        'lambda_q1': nrm(ks[12], (DEPTH, DH_B), 0.1),
        'lambda_k1': nrm(ks[13], (DEPTH, DH_B), 0.1),
        'lambda_q2': nrm(ks[14], (DEPTH, DH_B), 0.1),
        'lambda_k2': nrm(ks[15], (DEPTH, DH_B), 0.1),
        'diff_norm_w': 1.0 + nrm(ks[16], (DEPTH, DV_B), 0.02),
        'w_o': nrm(ks[17], (DEPTH, MIX_W, D_MODEL), BETA * MIX_W ** -0.5),
        'ln1_g': 1.0 + nrm(ks[18], (DEPTH, D_MODEL), 0.02),
        'ln1_b': nrm(ks[19], (DEPTH, D_MODEL), 0.02),
        'w_up': nrm(ks[20], (DEPTH, D_MODEL, 2 * D_FF), D_MODEL ** -0.5),
        'conv_w': nrm(ks[21], (DEPTH, CONV_W, D_FF), CONV_W ** -0.5),
        'conv_b': nrm(ks[22], (DEPTH, D_FF), 0.02),
        'w_down': nrm(ks[23], (DEPTH, D_FF, D_MODEL), BETA * D_FF ** -0.5),
        'ln2_g': 1.0 + nrm(ks[24], (DEPTH, D_MODEL), 0.02),
        'ln2_b': nrm(ks[25], (DEPTH, D_MODEL), 0.02),
    }


def reference(x_prompt, x_sample, cache_k, cache_v, state_hgrn, state_ffn_conv, page_table,
              ln_in_g, ln_in_b, w_in, hgrn_lb_logits, hgrn_norm_w,
              lambda_q1, lambda_k1, lambda_q2, lambda_k2, diff_norm_w, w_o,
              ln1_g, ln1_b, w_up, conv_w, conv_b, w_down, ln2_g, ln2_b):
    lbs = _lower_bounds(hgrn_lb_logits)
    xp = _layernorm(x_prompt, ln_in_g, ln_in_b)
    xs = _layernorm(x_sample, ln_in_g, ln_in_b)
    n_pb = x_prompt.shape[0]
    n_dec = x_sample.shape[0]
    conv_zero = jnp.zeros((n_pb, CONV_W - 1, D_FF), x_prompt.dtype)
    s_zero = jnp.zeros((n_pb, H_A, DK_A, DV_A), jnp.float32)
    k_p, v_p, k_s, v_s, s_p, s_s, c_p, c_s = [], [], [], [], [], [], [], []
    for l in range(DEPTH):
        lam_init = 0.8 - 0.6 * math.exp(-0.3 * l)
        lw = (w_in[l], hgrn_norm_w[l], lambda_q1[l], lambda_k1[l], lambda_q2[l], lambda_k2[l],
              diff_norm_w[l], w_o[l], ln1_g[l], ln1_b[l], w_up[l], conv_w[l], conv_b[l],
              w_down[l], ln2_g[l], ln2_b[l])
        xp, kp, vp, sp, cp = _layer(xp, 0, None, None, s_zero, conv_zero, lbs[l], lam_init, *lw)
        past_k = cache_k[l][page_table].reshape(n_dec, -1, H_B, 2 * DH_B)
        past_v = cache_v[l][page_table].reshape(n_dec, -1, H_B, DV_B)
        xs, ksl, vsl, ssl, csl = _layer(xs, past_k.shape[1], past_k, past_v, state_hgrn[l],
                                        state_ffn_conv[l], lbs[l], lam_init, *lw)
        k_p.append(kp); v_p.append(vp); s_p.append(sp); c_p.append(cp)
        k_s.append(ksl); v_s.append(vsl); s_s.append(ssl); c_s.append(csl)
    return (xp, xs, jnp.stack(k_p), jnp.stack(v_p), jnp.stack(k_s), jnp.stack(v_s),
            jnp.stack(s_p), jnp.stack(s_s), jnp.stack(c_p), jnp.stack(c_s))
```

```python
import functools
import math

import jax
import jax.numpy as jnp
from jax import lax
from jax.experimental import pallas as pl
from jax.experimental.pallas import tpu as pltpu

F32 = jnp.float32
BF16 = jnp.bfloat16

D_MODEL = 1024
DEPTH = 4
H_A, DK_A, DV_A = 4, 128, 128
H_B, DH_B, DV_B = 4, 64, 128
W_A = H_A * DV_A
W_B = H_B * DV_B
IN_COLS = 2 * H_A * DK_A + 2 * W_A + 4 * H_B * DH_B + W_B
D_FF = 2816
FF_SPLIT = 2
FF_PART = D_FF // FF_SPLIT
CONV_W = 3
CHUNK = 64
SUB = 16
ALPHA = (2 * DEPTH) ** 0.25
LN_EPS = 1e-5
RMS_EPS = 1e-6
NEG = -1e30
SAMPLE_PAD_T = 8
VMEM_LIMIT = 56 * 1024 * 1024


def _cparams(sem):
    return pltpu.CompilerParams(dimension_semantics=sem, vmem_limit_bytes=VMEM_LIMIT)


def _layernorm(x, g, b):
    mu = jnp.mean(x, axis=-1, keepdims=True)
    xc = x - mu
    var = jnp.mean(xc * xc, axis=-1, keepdims=True)
    return xc * lax.rsqrt(var + LN_EPS) * g + b


def _silu(x):
    return x / (1.0 + jnp.exp(-x))


def _dot(a, b):
    return jnp.dot(a, b, preferred_element_type=F32)


def _dot_nt(a, b):
    return lax.dot_general(a, b, (((1,), (1,)), ((), ())), preferred_element_type=F32)


def _dot_tn(a, b):
    return lax.dot_general(a, b, (((0,), (0,)), ((), ())), preferred_element_type=F32)


def _lambda(lam_ref, lam_init):
    lv = lam_ref[...]
    p1 = jnp.sum(lv[0:1] * lv[1:2], axis=-1, keepdims=True)
    p2 = jnp.sum(lv[2:3] * lv[3:4], axis=-1, keepdims=True)
    return jnp.exp(p1) - jnp.exp(p2) + lam_init


def _ln_kernel(x_ref, g_ref, b_ref, o_ref):
    o_ref[...] = _layernorm(x_ref[...], g_ref[...], b_ref[...])


def _ln_in(x, g, b):
    n = x.shape[0]
    tm = min(512, n)
    return pl.pallas_call(
        _ln_kernel,
        grid=(n // tm,),
        in_specs=[pl.BlockSpec((tm, D_MODEL), lambda i: (i, 0)),
                  pl.BlockSpec((1, D_MODEL), lambda i: (0, 0)),
                  pl.BlockSpec((1, D_MODEL), lambda i: (0, 0))],
        out_specs=pl.BlockSpec((tm, D_MODEL), lambda i: (i, 0)),
        out_shape=jax.ShapeDtypeStruct((n, D_MODEL), F32),
        compiler_params=_cparams(("parallel",)),
        name="ln_in",
    )(x, g.reshape(1, D_MODEL), b.reshape(1, D_MODEL))


def _inproj_kernel(x_ref, w_ref, lbl_ref, qa_ref, logf_ref, kc_ref, ia_ref, sg_ref,
                   qb_ref, kb_ref, vb_ref, kb16_ref, vb16_ref, *, layer):
    x = x_ref[...].astype(BF16)

    def proj(c0, n):
        return _dot(x, w_ref[:, c0:c0 + n])

    qa = proj(0, W_A)
    qa_ref[...] = _silu(qa) * (DK_A ** -0.5)

    fa = proj(W_A, W_A)
    log_sig = jnp.minimum(fa, 0.0) - jnp.log1p(jnp.exp(-jnp.abs(fa)))
    sig_neg = 1.0 / (1.0 + jnp.exp(fa))
    if layer == 0:
        logf_ref[...] = log_sig
        kc_ref[...] = sig_neg
    else:
        lg = lbl_ref[...]
        e = jnp.exp(lg - jnp.max(lg, axis=0, keepdims=True))
        sm = e / jnp.sum(e, axis=0, keepdims=True)
        lb = sm[1:2]
        for j in range(2, layer + 1):
            lb = lb + sm[j:j + 1]
        u = jnp.log(lb)
        w = jnp.log1p(-lb) + log_sig
        logf_ref[...] = jnp.maximum(u, w) + jnp.log1p(jnp.exp(-jnp.abs(u - w)))
        kc_ref[...] = (1.0 - lb) * sig_neg

    ia_ref[...] = proj(2 * W_A, W_A)
    sg_ref[...] = _silu(proj(3 * W_A, W_A))
    qb_ref[...] = (proj(4 * W_A, W_B) * (DH_B ** -0.5)).astype(qb_ref.dtype)
    kb = proj(4 * W_A + W_B, W_B)
    kb_ref[...] = kb
    kb16_ref[...] = kb.astype(BF16)
    vb = proj(4 * W_A + 2 * W_B, W_B)
    vb_ref[...] = vb
    vb16_ref[...] = vb.astype(BF16)


def _inproj(x, w_in16, lb_logits, layer, qb_dtype):
    n = x.shape[0]
    tm = min(256, n)
    row = lambda i: (i, 0)
    f32_out = jax.ShapeDtypeStruct((n, W_A), F32)
    bf_out = jax.ShapeDtypeStruct((n, W_A), BF16)
    blk = pl.BlockSpec((tm, W_A), row)
    return pl.pallas_call(
        functools.partial(_inproj_kernel, layer=layer),
        grid=(n // tm,),
        in_specs=[pl.BlockSpec((tm, D_MODEL), row),
                  pl.BlockSpec((None, D_MODEL, IN_COLS), lambda i: (layer, 0, 0)),
                  pl.BlockSpec((DEPTH, W_A), lambda i: (0, 0))],
        out_specs=[blk] * 10,
        out_shape=[f32_out] * 5 + [jax.ShapeDtypeStruct((n, W_B), qb_dtype),
                                   f32_out, f32_out, bf_out, bf_out],
        compiler_params=_cparams(("parallel",)),
        name="inproj",
    )(x, w_in16, lb_logits)


def _hgrn_kernel(q_ref, g_ref, kc_ref, v_ref, sg_ref, s0_ref, nw_ref, o_ref, s_ref, st_scr,
                 *, chunk, sub, t_valid, t_total):
    ci = pl.program_id(1)
    n_sub = chunk // sub

    @pl.when(ci == 0)
    def _():
        for h in range(H_A):
            st_scr[h] = s0_ref[h].T

    rows = lax.broadcasted_iota(jnp.int32, (chunk, 1), 0)
    tri = (lax.broadcasted_iota(jnp.int32, (chunk, chunk), 1)
           <= lax.broadcasted_iota(jnp.int32, (chunk, chunk), 0)).astype(F32)
    sub_rows = lax.broadcasted_iota(jnp.int32, (sub, 1), 0)
    nw = nw_ref[...]

    for h in range(H_A):
        hs = slice(h * DK_A, (h + 1) * DK_A)
        q = q_ref[:, hs]
        g = g_ref[:, hs]
        kc = kc_ref[:, hs]
        v = v_ref[:, hs]
        if t_valid < t_total:
            valid = (ci * chunk + rows) < t_valid
            g = jnp.where(valid, g, 0.0)
            kc = jnp.where(valid, kc, 0.0)
        b = jnp.dot(tri, g, preferred_element_type=F32, precision=lax.Precision.HIGHEST)
        st = st_scr[h]
        v16 = v.astype(BF16)
        o_inter = _dot_nt((q * jnp.exp(b)).astype(BF16), st.astype(BF16))
        outs = []
        for i in range(n_sub):
            r0 = i * sub
            bi = b[r0:r0 + sub]
            qi = q[r0:r0 + sub]
            kci = kc[r0:r0 + sub]
            vi = v[r0:r0 + sub]
            oi = o_inter[r0:r0 + sub]
            if i > 0:
                ri = b[r0 - 1:r0]
                qs = (qi * jnp.exp(bi - ri)).astype(BF16)
                ks = (kc[:r0] * jnp.exp(ri - b[:r0])).astype(BF16)
                a_off = _dot_nt(qs, ks)
                oi = oi + _dot(a_off.astype(BF16), v16[:r0])
            od = jnp.zeros((sub, DV_A), F32)
            for t in range(sub):
                d = jnp.exp(jnp.minimum(bi[t:t + 1] - bi, 0.0))
                a = jnp.sum(qi[t:t + 1] * kci * d, axis=-1, keepdims=True)
                a = jnp.where(sub_rows <= t, a, 0.0)
                ot = jnp.sum(a * vi, axis=0, keepdims=True)
                od = jnp.where(sub_rows == t, ot, od)
            outs.append(oi + od)
        o = jnp.concatenate(outs, axis=0) if n_sub > 1 else outs[0]

        b_last = b[chunk - 1:chunk]
        k_last = (kc * jnp.exp(b_last - b)).astype(BF16)
        st_new = st * jnp.exp(b_last) + _dot_tn(v16, k_last)
        st_scr[h] = st_new

        ms = jnp.mean(o * o, axis=-1, keepdims=True)
        o_ref[:, hs] = (o * lax.rsqrt(ms + RMS_EPS) * nw * sg_ref[:, hs]).astype(o_ref.dtype)

        @pl.when(ci == pl.num_programs(1) - 1)
        def _():
            s_ref[h] = st_new.T


def _hgrn(qa, logf, kc, ia, sg, s0, norm_w, layer, t_valid):
    bsz, t, _ = qa.shape
    chunk = CHUNK if t % CHUNK == 0 else t
    sub = min(SUB, chunk)
    seq = pl.BlockSpec((None, chunk, W_A), lambda b, c: (b, c, 0))
    state = pl.BlockSpec((None, H_A, DK_A, DV_A), lambda b, c: (b, 0, 0, 0))
    return pl.pallas_call(
        functools.partial(_hgrn_kernel, chunk=chunk, sub=sub, t_valid=t_valid, t_total=t),
        grid=(bsz, t // chunk),
        in_specs=[seq, seq, seq, seq, seq, state,
                  pl.BlockSpec((None, 1, DV_A), lambda b, c: (layer, 0, 0))],
        out_specs=[seq, state],
        out_shape=[jax.ShapeDtypeStruct((bsz, t, W_A), BF16),
                   jax.ShapeDtypeStruct((bsz, H_A, DK_A, DV_A), F32)],
        scratch_shapes=[pltpu.VMEM((H_A, DV_A, DK_A), F32)],
        compiler_params=_cparams(("parallel", "arbitrary")),
        name="hgrn2",
    )(qa, logf, kc, ia, sg, s0, norm_w)


def _head_slope(h):
    return jnp.where(h == 0, 2.0 ** -2, jnp.where(h == 1, 2.0 ** -4,
                     jnp.where(h == 2, 2.0 ** -6, 2.0 ** -8))).astype(F32)


def _softmax_update(s, v16, m_scr, l_scr, acc_scr):
    m_prev = m_scr[...]
    m_new = jnp.maximum(m_prev, jnp.max(s, axis=-1, keepdims=True))
    alpha = jnp.exp(m_prev - m_new)
    p = jnp.exp(s - m_new)
    l_scr[...] = alpha * l_scr[...] + jnp.sum(p, axis=-1, keepdims=True)
    acc_scr[...] = alpha * acc_scr[...] + _dot(p.astype(BF16), v16)
    m_scr[...] = m_new


def _attn_kernel(q_ref, k_ref, v_ref, lam_ref, nw_ref, o_ref, m_scr, l_scr, acc_scr,
                 *, tq, lam_init):
    h = pl.program_id(1)
    i = pl.program_id(2)
    qf = q_ref[...].astype(F32)
    lane = lax.broadcasted_iota(jnp.int32, (tq, 2 * DH_B), 1)
    q2 = jnp.concatenate([jnp.where(lane < DH_B, qf, 0.0),
                          jnp.where(lane >= DH_B, qf, 0.0)], axis=0).astype(BF16)
    slope = _head_slope(h)
    k_iota = lax.broadcasted_iota(jnp.int32, (1, tq), 1)
    q_rel = lax.broadcasted_iota(jnp.int32, (2 * tq, 1), 0) % tq

    m_scr[...] = jnp.full(m_scr.shape, NEG, F32)
    l_scr[...] = jnp.zeros(l_scr.shape, F32)
    acc_scr[...] = jnp.zeros(acc_scr.shape, F32)

    def block(j, masked):
        start = pl.multiple_of(j * tq, tq)
        k = k_ref[pl.ds(start, tq), :]
        v = v_ref[pl.ds(start, tq), :]
        s = _dot_nt(q2, k)
        s = s + slope * (k_iota + (j - i) * tq).astype(F32)
        if masked:
            s = jnp.where(k_iota <= q_rel, s, NEG)
        _softmax_update(s, v, m_scr, l_scr, acc_scr)

    def body(j, carry):
        block(j, False)
        return carry

    lax.fori_loop(0, i, body, 0)
    block(i, True)

    lam = _lambda(lam_ref, lam_init)
    o1 = acc_scr[0:tq] / l_scr[0:tq]
    o2 = acc_scr[tq:2 * tq] / l_scr[tq:2 * tq]
    o = o1 - lam * o2
    ms = jnp.mean(o * o, axis=-1, keepdims=True)
    o_ref[...] = (o * lax.rsqrt(ms + RMS_EPS) * nw_ref[...] * (1.0 - lam_init)).astype(o_ref.dtype)


def _attn_prompt(qb16, kb16, vb16, lam_vecs, norm_w, layer, lam_init):
    bsz, t, _ = qb16.shape
    tq = min(256, t)
    return pl.pallas_call(
        functools.partial(_attn_kernel, tq=tq, lam_init=lam_init),
        grid=(bsz, H_B, t // tq),
        in_specs=[pl.BlockSpec((None, tq, DV_B), lambda b, h, i: (b, i, h)),
                  pl.BlockSpec((None, t, DV_B), lambda b, h, i: (b, 0, h)),
                  pl.BlockSpec((None, t, DV_B), lambda b, h, i: (b, 0, h)),
                  pl.BlockSpec((None, 4, DH_B), lambda b, h, i: (layer, 0, 0)),
                  pl.BlockSpec((None, 1, DV_B), lambda b, h, i: (layer, 0, 0))],
        out_specs=pl.BlockSpec((None, tq, DV_B), lambda b, h, i: (b, i, h)),
        out_shape=jax.ShapeDtypeStruct((bsz, t, W_B), BF16),
        scratch_shapes=[pltpu.VMEM((2 * tq, 1), F32), pltpu.VMEM((2 * tq, 1), F32),
                        pltpu.VMEM((2 * tq, DV_B), F32)],
        compiler_params=_cparams(("parallel", "parallel", "arbitrary")),
        name="diff_attn_prompt",
    )(qb16, kb16, vb16, lam_vecs, norm_w)


def _dec_kernel(pt_ref, q_ref, kn_ref, vn_ref, lam_ref, nw_ref, *rest,
                n_pg, page, past_len, lam_init, t_valid):
    del pt_ref
    k_refs = rest[:n_pg]
    v_refs = rest[n_pg:2 * n_pg]
    o_ref = rest[2 * n_pg]
    qall_scr, m_scr, l_scr, acc_scr, k16_scr, v16_scr = rest[2 * n_pg + 1:]
    j = pl.program_id(1)
    tp = SAMPLE_PAD_T
    n_rows = H_B * 2 * tp
    row = lax.broadcasted_iota(jnp.int32, (n_rows, 1), 0)
    slope = jnp.where(row < 2 * tp, 2.0 ** -2, jnp.where(row < 4 * tp, 2.0 ** -4,
                      jnp.where(row < 6 * tp, 2.0 ** -6, 2.0 ** -8))).astype(F32)

    @pl.when(j == 0)
    def _():
        qf = q_ref[...]
        col = lax.broadcasted_iota(jnp.int32, (tp, W_B), 1)
        parts = []
        for h in range(H_B):
            for m in range(2):
                lo = h * 2 * DH_B + m * DH_B
                parts.append(jnp.where((col >= lo) & (col < lo + DH_B), qf, 0.0))
        qall_scr[...] = jnp.concatenate(parts, axis=0).astype(BF16)
        m_scr[...] = jnp.full(m_scr.shape, NEG, F32)
        l_scr[...] = jnp.zeros(l_scr.shape, F32)
        acc_scr[...] = jnp.zeros(acc_scr.shape, F32)

    for p in range(n_pg):
        k16_scr[p * page:(p + 1) * page, :] = k_refs[p][...].astype(BF16)
        v16_scr[p * page:(p + 1) * page, :] = v_refs[p][...].astype(BF16)
    qall = qall_scr[...]
    s = _dot_nt(qall, k16_scr[...])
    k_pos = j * (n_pg * page) + lax.broadcasted_iota(jnp.int32, (1, n_pg * page), 1)
    s = s + slope * (k_pos - past_len).astype(F32)
    _softmax_update(s, v16_scr[...], m_scr, l_scr, acc_scr)

    @pl.when(j == pl.num_programs(1) - 1)
    def _():
        s_n = _dot_nt(qall, kn_ref[...].astype(BF16))
        t_k = lax.broadcasted_iota(jnp.int32, (1, page), 1)
        t_q = row % tp
        s_n = jnp.where((t_k <= t_q) & (t_k < t_valid), s_n + slope * t_k.astype(F32), NEG)
        _softmax_update(s_n, vn_ref[...].astype(BF16), m_scr, l_scr, acc_scr)
        lam = _lambda(lam_ref, lam_init)
        nw = nw_ref[...]
        for h in range(H_B):
            r0 = h * 2 * tp
            acc_h = acc_scr[r0:r0 + 2 * tp, h * DV_B:(h + 1) * DV_B]
            l_h = l_scr[r0:r0 + 2 * tp]
            o = acc_h[0:tp] / l_h[0:tp] - lam * (acc_h[tp:2 * tp] / l_h[tp:2 * tp])
            ms = jnp.mean(o * o, axis=-1, keepdims=True)
            o_ref[:, h * DV_B:(h + 1) * DV_B] = (
                o * lax.rsqrt(ms + RMS_EPS) * nw * (1.0 - lam_init)).astype(o_ref.dtype)


def _attn_sample(qb, kn_pad, vn_pad, cache_k, cache_v, page_table, lam_vecs, norm_w,
                 layer, lam_init, t_valid):
    bsz, tp, _ = qb.shape
    n_pages = page_table.shape[1]
    page = cache_k.shape[2]
    n_pg = 8 if n_pages % 8 == 0 else 1
    n_rows = H_B * 2 * tp

    def page_spec(p):
        return pl.BlockSpec((None, None, page, W_B),
                            lambda b, j, pt: (layer, pt[b, j * n_pg + p], 0, 0))

    grid_spec = pltpu.PrefetchScalarGridSpec(
        num_scalar_prefetch=1,
        grid=(bsz, n_pages // n_pg),
        in_specs=[pl.BlockSpec((None, tp, W_B), lambda b, j, pt: (b, 0, 0)),
                  pl.BlockSpec((None, page, W_B), lambda b, j, pt: (b, 0, 0)),
                  pl.BlockSpec((None, page, W_B), lambda b, j, pt: (b, 0, 0)),
                  pl.BlockSpec((None, 4, DH_B), lambda b, j, pt: (layer, 0, 0)),
                  pl.BlockSpec((None, 1, DV_B), lambda b, j, pt: (layer, 0, 0))]
                 + [page_spec(p) for p in range(n_pg)] * 2,
        out_specs=pl.BlockSpec((None, tp, W_B), lambda b, j, pt: (b, 0, 0)),
        scratch_shapes=[pltpu.VMEM((n_rows, W_B), BF16),
                        pltpu.VMEM((n_rows, 1), F32), pltpu.VMEM((n_rows, 1), F32),
                        pltpu.VMEM((n_rows, W_B), F32),
                        pltpu.VMEM((n_pg * page, W_B), BF16),
                        pltpu.VMEM((n_pg * page, W_B), BF16)],
    )
    return pl.pallas_call(
        functools.partial(_dec_kernel, n_pg=n_pg, page=page, past_len=n_pages * page,
                          lam_init=lam_init, t_valid=t_valid),
        grid_spec=grid_spec,
        out_shape=jax.ShapeDtypeStruct((bsz, tp, W_B), F32),
        compiler_params=_cparams(("parallel", "arbitrary")),
        name="diff_attn_sample",
    )(page_table, qb, kn_pad, vn_pad, lam_vecs, norm_w,
      *([cache_k] * n_pg), *([cache_v] * n_pg))


def _oproj_kernel(oa_ref, ob_ref, x_ref, w_ref, g_ref, b_ref, o_ref):
    mixed = (_dot(oa_ref[...].astype(BF16), w_ref[0:W_A, :])
             + _dot(ob_ref[...].astype(BF16), w_ref[W_A:W_A + W_B, :]))
    o_ref[...] = _layernorm(ALPHA * x_ref[...] + mixed, g_ref[...], b_ref[...])


def _oproj(oa, ob, x, w_o16, ln_g, ln_b, layer):
    n = x.shape[0]
    tm = min(512, n)
    row = lambda i: (i, 0)
    vec = pl.BlockSpec((None, 1, D_MODEL), lambda i: (layer, 0, 0))
    return pl.pallas_call(
        _oproj_kernel,
        grid=(n // tm,),
        in_specs=[pl.BlockSpec((tm, W_A), row), pl.BlockSpec((tm, W_B), row),
                  pl.BlockSpec((tm, D_MODEL), row),
                  pl.BlockSpec((None, D_MODEL, D_MODEL), lambda i: (layer, 0, 0)),
                  vec, vec],
        out_specs=pl.BlockSpec((tm, D_MODEL), row),
        out_shape=jax.ShapeDtypeStruct((n, D_MODEL), F32),
        compiler_params=_cparams(("parallel",)),
        name="oproj_ln",
    )(oa, ob, x, w_o16, ln_g, ln_b)


def _ffn_kernel(*refs, tm, seq_len, per_row_state):
    if per_row_state:
        (x_ref, wup_ref, cw_ref, cb_ref, wdn_ref, g_ref, b_ref, p1_ref, p2_ref,
         o_ref, a_ref) = refs
    else:
        (x_ref, wup_ref, cw_ref, cb_ref, wdn_ref, g_ref, b_ref,
         o_ref, cs_ref, carry_scr) = refs
        ti = pl.program_id(1)

        @pl.when(ti == 0)
        def _():
            carry_scr[...] = jnp.zeros(carry_scr.shape, F32)

    x = x_ref[...]
    x16 = x.astype(BF16)
    row = lax.broadcasted_iota(jnp.int32, (tm, 1), 0)
    y = jnp.zeros((tm, D_MODEL), F32)
    for part in range(FF_SPLIT):
        cs = slice(part * FF_PART, (part + 1) * FF_PART)
        a = _dot(x16, wup_ref[:, cs])
        g = _dot(x16, wup_ref[:, D_FF + part * FF_PART:D_FF + (part + 1) * FF_PART])
        r1 = pltpu.roll(a, 1, 0)
        r2 = pltpu.roll(a, 2, 0)
        if per_row_state:
            t = row % seq_len
            am1 = jnp.where(t >= 1, r1, p1_ref[:, cs])
            am2 = jnp.where(t >= 2, r2, p2_ref[:, cs])
            a_ref[:, cs] = a
        else:
            last = carry_scr[1:2, cs]
            prev = carry_scr[0:1, cs]
            am1 = jnp.where(row == 0, last, r1)
            am2 = jnp.where(row == 0, prev, jnp.where(row == 1, last, r2))
            carry_scr[:, cs] = a[tm - 2:tm]

            @pl.when(ti == pl.num_programs(1) - 1)
            def _():
                cs_ref[:, cs] = a[tm - 2:tm]

        cw = cw_ref[:, cs]
        c = cb_ref[:, cs] + am2 * cw[0:1] + am1 * cw[1:2] + a * cw[2:3]
        y = y + _dot((_silu(c) * g).astype(BF16), wdn_ref[cs, :])
    o_ref[...] = _layernorm(ALPHA * x + y, g_ref[...], b_ref[...])


def _ffn_specs(layer, idx):
    return [pl.BlockSpec((None, D_MODEL, 2 * D_FF), lambda *a: (layer, 0, 0)),
            pl.BlockSpec((None, CONV_W, D_FF), lambda *a: (layer, 0, 0)),
            pl.BlockSpec((None, 1, D_FF), lambda *a: (layer, 0, 0)),
            pl.BlockSpec((None, D_FF, D_MODEL), lambda *a: (layer, 0, 0)),
            pl.BlockSpec((None, 1, D_MODEL), lambda *a: (layer, 0, 0)),
            pl.BlockSpec((None, 1, D_MODEL), lambda *a: (layer, 0, 0))]


def _ffn_prompt(x, w_up16, conv_w, conv_b, w_dn16, ln_g, ln_b, layer):
    bsz, t, _ = x.shape
    tm = min(256, t)
    return pl.pallas_call(
        functools.partial(_ffn_kernel, tm=tm, seq_len=t, per_row_state=False),
        grid=(bsz, t // tm),
        in_specs=[pl.BlockSpec((None, tm, D_MODEL), lambda b, i: (b, i, 0))] + _ffn_specs(layer, 2),
        out_specs=[pl.BlockSpec((None, tm, D_MODEL), lambda b, i: (b, i, 0)),
                   pl.BlockSpec((None, CONV_W - 1, D_FF), lambda b, i: (b, 0, 0))],
        out_shape=[jax.ShapeDtypeStruct((bsz, t, D_MODEL), F32),
                   jax.ShapeDtypeStruct((bsz, CONV_W - 1, D_FF), F32)],
        scratch_shapes=[pltpu.VMEM((CONV_W - 1, D_FF), F32)],
        compiler_params=_cparams(("parallel", "arbitrary")),
        name="convffn_prompt",
    )(x, w_up16, conv_w, conv_b, w_dn16, ln_g, ln_b)


def _ffn_sample(x, w_up16, conv_w, conv_b, w_dn16, ln_g, ln_b, p1, p2, layer, seq_len):
    n = x.shape[0]
    full = lambda i: (0, 0)
    return pl.pallas_call(
        functools.partial(_ffn_kernel, tm=n, seq_len=seq_len, per_row_state=True),
        grid=(1,),
        in_specs=[pl.BlockSpec((n, D_MODEL), full)] + _ffn_specs(layer, 1)
                 + [pl.BlockSpec((n, D_FF), full), pl.BlockSpec((n, D_FF), full)],
        out_specs=[pl.BlockSpec((n, D_MODEL), full), pl.BlockSpec((n, D_FF), full)],
        out_shape=[jax.ShapeDtypeStruct((n, D_MODEL), F32),
                   jax.ShapeDtypeStruct((n, D_FF), F32)],
        compiler_params=_cparams(("arbitrary",)),
        name="convffn_sample",
    )(x, w_up16, conv_w, conv_b, w_dn16, ln_g, ln_b, p1, p2)


def kernel(x_prompt, x_sample, cache_k, cache_v, state_hgrn, state_ffn_conv, page_table,
           ln_in_g, ln_in_b, w_in, hgrn_lb_logits, hgrn_norm_w,
           lambda_q1, lambda_k1, lambda_q2, lambda_k2, diff_norm_w, w_o,
           ln1_g, ln1_b, w_up, conv_w, conv_b, w_down, ln2_g, ln2_b):
    bp, tp_, _ = x_prompt.shape
    bs, ts, _ = x_sample.shape
    pad_t = SAMPLE_PAD_T
    n_phys, page = cache_k.shape[1], cache_k.shape[2]
    np_rows, ns_rows = bp * tp_, bs * pad_t

    w_in16 = w_in.astype(BF16)
    w_o16 = w_o.astype(BF16)
    w_up16 = w_up.astype(BF16)
    w_dn16 = w_down.astype(BF16)
    lam_vecs = jnp.stack([lambda_q1, lambda_k1, lambda_q2, lambda_k2], axis=1)
    ck = cache_k.reshape(DEPTH, n_phys, page, W_B)
    cv = cache_v.reshape(DEPTH, n_phys, page, W_B)
    s_zero = jnp.zeros((bp, H_A, DK_A, DV_A), F32)
    per_layer = lambda a: a.reshape(DEPTH, 1, a.shape[-1])
    hgrn_norm_w, diff_norm_w, conv_b = per_layer(hgrn_norm_w), per_layer(diff_norm_w), per_layer(conv_b)
    ln1_g, ln1_b, ln2_g, ln2_b = per_layer(ln1_g), per_layer(ln1_b), per_layer(ln2_g), per_layer(ln2_b)

    xp = _ln_in(x_prompt.reshape(np_rows, D_MODEL), ln_in_g, ln_in_b)
    xs = _ln_in(jnp.pad(x_sample, ((0, 0), (0, pad_t - ts), (0, 0))).reshape(ns_rows, D_MODEL),
                ln_in_g, ln_in_b)

    k_p, v_p, k_s, v_s, s_p, s_s, c_p, c_s = [], [], [], [], [], [], [], []
    for l in range(DEPTH):
        lam_init = 0.8 - 0.6 * math.exp(-0.3 * l)

        qa, logf, kc, ia, sg, qb, kb, vb, kb16, vb16 = _inproj(xp, w_in16, hgrn_lb_logits, l, BF16)
        seq = lambda a: a.reshape(bp, tp_, a.shape[-1])
        oa, sp = _hgrn(seq(qa), seq(logf), seq(kc), seq(ia), seq(sg), s_zero, hgrn_norm_w, l, tp_)
        ob = _attn_prompt(seq(qb), seq(kb16), seq(vb16), lam_vecs, diff_norm_w, l, lam_init)
        x1 = _oproj(oa.reshape(np_rows, W_A), ob.reshape(np_rows, W_B), xp, w_o16, ln1_g, ln1_b, l)
        x2, cp = _ffn_prompt(seq(x1), w_up16, conv_w, conv_b, w_dn16, ln2_g, ln2_b, l)
        xp = x2.reshape(np_rows, D_MODEL)
        k_p.append(kb.reshape(bp, tp_, H_B, 2 * DH_B))
        v_p.append(vb.reshape(bp, tp_, H_B, DV_B))
        s_p.append(sp)
        c_p.append(cp)

        qa, logf, kc, ia, sg, qb, kb, vb, _, _ = _inproj(xs, w_in16, hgrn_lb_logits, l, F32)
        seq = lambda a: a.reshape(bs, pad_t, a.shape[-1])
        oa, ss = _hgrn(seq(qa), seq(logf), seq(kc), seq(ia), seq(sg), state_hgrn[l],
                       hgrn_norm_w, l, ts)
        grow = lambda a: jnp.pad(seq(a), ((0, 0), (0, page - pad_t), (0, 0)))
        ob = _attn_sample(seq(qb), grow(kb), grow(vb), ck, cv, page_table, lam_vecs,
                          diff_norm_w, l, lam_init, ts)
        x1 = _oproj(oa.reshape(ns_rows, W_A), ob.reshape(ns_rows, W_B), xs, w_o16, ln1_g, ln1_b, l)
        conv0 = state_ffn_conv[l]
        p1 = jnp.pad(conv0[:, 1:2], ((0, 0), (0, pad_t - 1), (0, 0))).reshape(ns_rows, D_FF)
        p2 = jnp.pad(conv0, ((0, 0), (0, pad_t - 2), (0, 0))).reshape(ns_rows, D_FF)
        xs, a_s = _ffn_sample(x1, w_up16, conv_w, conv_b, w_dn16, ln2_g, ln2_b, p1, p2, l, pad_t)
        k_s.append(seq(kb)[:, :ts].reshape(bs, ts, H_B, 2 * DH_B))
        v_s.append(seq(vb)[:, :ts].reshape(bs, ts, H_B, DV_B))
        s_s.append(ss)
        c_s.append(a_s.reshape(bs, pad_t, D_FF)[:, ts - (CONV_W - 1):ts])

    y_p = xp.reshape(bp, tp_, D_MODEL)
    y_s = xs.reshape(bs, pad_t, D_MODEL)[:, :ts]
    return (y_p, y_s, jnp.stack(k_p), jnp.stack(v_p), jnp.stack(k_s), jnp.stack(v_s),
            jnp.stack(s_p), jnp.stack(s_s), jnp.stack(c_p), jnp.stack(c_s))
```

```python
import functools
import math

import jax
import jax.numpy as jnp
from jax import lax
from jax.experimental import pallas as pl
from jax.experimental.pallas import tpu as pltpu

F32 = jnp.float32
BF16 = jnp.bfloat16

D_MODEL = 1024
DEPTH = 4
H_A, DK_A, DV_A = 4, 128, 128
H_B, DH_B, DV_B = 4, 64, 128
W_A = H_A * DV_A
W_B = H_B * DV_B
IN_COLS = 2 * H_A * DK_A + 2 * W_A + 4 * H_B * DH_B + W_B
D_FF = 2816
FF_SPLIT = 2
FF_PART = D_FF // FF_SPLIT
CONV_W = 3
CHUNK = 128
ALPHA = (2 * DEPTH) ** 0.25
LN_EPS = 1e-5
RMS_EPS = 1e-6
LOG2_E = 1.0 / math.log(2.0)
NEG = -1e30
SAMPLE_PAD_T = 8
VMEM_LIMIT = 56 * 1024 * 1024


def _cparams(sem):
    return pltpu.CompilerParams(dimension_semantics=sem, vmem_limit_bytes=VMEM_LIMIT)


def _layernorm(x, g, b):
    mu = jnp.mean(x, axis=-1, keepdims=True)
    xc = x - mu
    var = jnp.mean(xc * xc, axis=-1, keepdims=True)
    return xc * lax.rsqrt(var + LN_EPS) * g + b


def _silu(x):
    return x / (1.0 + jnp.exp(-x))


def _dot(a, b):
    return jnp.dot(a, b, preferred_element_type=F32)


def _dot_nt(a, b):
    return lax.dot_general(a, b, (((1,), (1,)), ((), ())), preferred_element_type=F32)


def _dot_tn(a, b):
    return lax.dot_general(a, b, (((0,), (0,)), ((), ())), preferred_element_type=F32)


def _lambda(lam_ref, lam_init):
    lv = lam_ref[...]
    p1 = jnp.sum(lv[0:1] * lv[1:2], axis=-1, keepdims=True)
    p2 = jnp.sum(lv[2:3] * lv[3:4], axis=-1, keepdims=True)
    return jnp.exp(p1) - jnp.exp(p2) + lam_init


def _diff_norm(acc1, l1, acc2, l2, lam, nw, lam_init):
    o = acc1 / l1 - lam * (acc2 / l2)
    ms = jnp.mean(o * o, axis=-1, keepdims=True)
    return o * lax.rsqrt(ms + RMS_EPS) * nw * (1.0 - lam_init)


def _ln_kernel(x_ref, g_ref, b_ref, o_ref):
    o_ref[...] = _layernorm(x_ref[...], g_ref[...], b_ref[...])


def _ln_in(x, g, b):
    n = x.shape[0]
    tm = min(512, n)
    return pl.pallas_call(
        _ln_kernel,
        grid=(n // tm,),
        in_specs=[pl.BlockSpec((tm, D_MODEL), lambda i: (i, 0)),
                  pl.BlockSpec((1, D_MODEL), lambda i: (0, 0)),
                  pl.BlockSpec((1, D_MODEL), lambda i: (0, 0))],
        out_specs=pl.BlockSpec((tm, D_MODEL), lambda i: (i, 0)),
        out_shape=jax.ShapeDtypeStruct((n, D_MODEL), F32),
        compiler_params=_cparams(("parallel",)),
        name="ln_in",
    )(x, g.reshape(1, D_MODEL), b.reshape(1, D_MODEL))


def _inproj_kernel(x_ref, w_ref, lbl_ref, qa_ref, logf_ref, kc_ref, ia_ref, sg_ref,
                   qb_ref, kb_ref, vb_ref, kb16_ref, vb16_ref, *, layer):
    x = x_ref[...].astype(BF16)

    def proj(c0, n):
        return _dot(x, w_ref[:, c0:c0 + n])

    qa = proj(0, W_A)
    qa_ref[...] = _silu(qa) * (DK_A ** -0.5)

    fa = proj(W_A, W_A)
    log_sig = jnp.minimum(fa, 0.0) - jnp.log1p(jnp.exp(-jnp.abs(fa)))
    sig_neg = 1.0 / (1.0 + jnp.exp(fa))
    if layer == 0:
        logf_ref[...] = log_sig
        kc_ref[...] = sig_neg
    else:
        lg = lbl_ref[...]
        e = jnp.exp(lg - jnp.max(lg, axis=0, keepdims=True))
        sm = e / jnp.sum(e, axis=0, keepdims=True)
        lb = sm[1:2]
        for j in range(2, layer + 1):
            lb = lb + sm[j:j + 1]
        u = jnp.log(lb)
        w = jnp.log1p(-lb) + log_sig
        logf_ref[...] = jnp.maximum(u, w) + jnp.log1p(jnp.exp(-jnp.abs(u - w)))
        kc_ref[...] = (1.0 - lb) * sig_neg

    ia_ref[...] = proj(2 * W_A, W_A)
    sg_ref[...] = _silu(proj(3 * W_A, W_A))
    qb_ref[...] = (proj(4 * W_A, W_B) * (DH_B ** -0.5)).astype(qb_ref.dtype)
    kb = proj(4 * W_A + W_B, W_B)
    kb_ref[...] = kb
    kb16_ref[...] = kb.astype(BF16)
    vb = proj(4 * W_A + 2 * W_B, W_B)
    vb_ref[...] = vb
    vb16_ref[...] = vb.astype(BF16)


def _inproj(x, w_in16, lb_logits, layer, qb_dtype):
    n = x.shape[0]
    tm = min(256, n)
    row = lambda i: (i, 0)
    f32_out = jax.ShapeDtypeStruct((n, W_A), F32)
    bf_out = jax.ShapeDtypeStruct((n, W_A), BF16)
    blk = pl.BlockSpec((tm, W_A), row)
    return pl.pallas_call(
        functools.partial(_inproj_kernel, layer=layer),
        grid=(n // tm,),
        in_specs=[pl.BlockSpec((tm, D_MODEL), row),
                  pl.BlockSpec((None, D_MODEL, IN_COLS), lambda i: (layer, 0, 0)),
                  pl.BlockSpec((DEPTH, W_A), lambda i: (0, 0))],
        out_specs=[blk] * 10,
        out_shape=[f32_out] * 5 + [jax.ShapeDtypeStruct((n, W_B), qb_dtype),
                                   f32_out, f32_out, bf_out, bf_out],
        compiler_params=_cparams(("parallel",)),
        name="inproj",
    )(x, w_in16, lb_logits)


def _boundary_rows(b, m, rows):
    c, w = b.shape
    if 2 * m >= 8:
        parts = [jnp.broadcast_to(b[p * 2 * m + m - 1:p * 2 * m + m], (2 * m, w))
                 for p in range(c // (2 * m))]
        return parts[0] if len(parts) == 1 else jnp.concatenate(parts, axis=0)
    if m == 2:
        r = rows % 4
        return jnp.where(r == 0, pltpu.roll(b, c - 1, 0),
                         jnp.where(r == 1, b,
                                   jnp.where(r == 2, pltpu.roll(b, 1, 0), pltpu.roll(b, 2, 0))))
    return jnp.where(rows % 2 == 0, b, pltpu.roll(b, 1, 0))


def _hgrn_kernel(q_ref, g_ref, kc_ref, v_ref, sg_ref, s0_ref, nw_ref, o_ref, s_ref, st_scr,
                 *, chunk, t_valid, t_total):
    ci = pl.program_id(1)

    @pl.when(ci == 0)
    def _():
        st_scr[...] = jnp.concatenate([s0_ref[h].T for h in range(H_A)], axis=1)

    rows = lax.broadcasted_iota(jnp.int32, (chunk, 1), 0)
    t_idx = lax.broadcasted_iota(jnp.int32, (chunk, chunk), 0)
    s_idx = lax.broadcasted_iota(jnp.int32, (chunk, chunk), 1)
    tri = (s_idx <= t_idx).astype(F32)

    q = q_ref[...]
    g = g_ref[...]
    kc = kc_ref[...]
    if t_valid < t_total:
        valid = (ci * chunk + rows) < t_valid
        g = jnp.where(valid, g, 0.0)
        kc = jnp.where(valid, kc, 0.0)
    v16 = v_ref[...].astype(BF16)
    heads = [slice(h * DK_A, (h + 1) * DK_A) for h in range(H_A)]

    b = jnp.dot(tri, g, preferred_element_type=F32, precision=lax.Precision.HIGHEST) * LOG2_E

    level = jnp.where(s_idx <= t_idx, 31 - lax.clz(t_idx ^ s_idx), -2)
    q16 = q.astype(BF16)
    k16 = kc.astype(BF16)
    a = [jnp.where(level == -1, _dot_nt(q16[:, hs], k16[:, hs]), 0.0) for hs in heads]
    m = chunk // 2
    while m >= 1:
        r = _boundary_rows(b, m, rows)
        second_half = (rows // m) % 2 == 1
        x = (jnp.where(second_half, q, kc) * jnp.exp2(-jnp.abs(b - r))).astype(BF16)
        keep = level == (m.bit_length() - 1)
        a = [jnp.where(keep, _dot_nt(x[:, hs], x[:, hs]), a[h]) for h, hs in enumerate(heads)]
        m //= 2

    qe = (q * jnp.exp2(b)).astype(BF16)
    b_last = b[chunk - 1:chunk]
    decay = jnp.exp2(b_last)
    k_last = (kc * jnp.exp2(b_last - b)).astype(BF16)
    st = st_scr[...]
    st16 = st.astype(BF16)
    o = [_dot_nt(qe[:, hs], st16[:, hs]) + _dot(a[h].astype(BF16), v16[:, hs])
         for h, hs in enumerate(heads)]
    st_new = st * decay + jnp.concatenate(
        [_dot_tn(v16[:, hs], k_last[:, hs]) for hs in heads], axis=1)
    st_scr[...] = st_new
    nw = nw_ref[...]
    scale = jnp.concatenate(
        [jnp.broadcast_to(lax.rsqrt(jnp.mean(oh * oh, axis=-1, keepdims=True) + RMS_EPS),
                          (chunk, DV_A)) * nw for oh in o], axis=1)
    o_ref[...] = (jnp.concatenate(o, axis=1) * scale * sg_ref[...]).astype(o_ref.dtype)

    @pl.when(ci == pl.num_programs(1) - 1)
    def _():
        for h, hs in enumerate(heads):
            s_ref[h] = st_new[:, hs].T


def _hgrn(qa, logf, kc, ia, sg, s0, norm_w, layer, t_valid):
    bsz, t, _ = qa.shape
    chunk = CHUNK if t % CHUNK == 0 else t
    seq = pl.BlockSpec((None, chunk, W_A), lambda b, c: (b, c, 0))
    state = pl.BlockSpec((None, H_A, DK_A, DV_A), lambda b, c: (b, 0, 0, 0))
    return pl.pallas_call(
        functools.partial(_hgrn_kernel, chunk=chunk, t_valid=t_valid, t_total=t),
        grid=(bsz, t // chunk),
        in_specs=[seq, seq, seq, seq, seq, state,
                  pl.BlockSpec((None, 1, DV_A), lambda b, c: (layer, 0, 0))],
        out_specs=[seq, state],
        out_shape=[jax.ShapeDtypeStruct((bsz, t, W_A), BF16),
                   jax.ShapeDtypeStruct((bsz, H_A, DK_A, DV_A), F32)],
        scratch_shapes=[pltpu.VMEM((DV_A, H_A * DK_A), F32)],
        compiler_params=_cparams(("parallel", "arbitrary")),
        name="hgrn2",
    )(qa, logf, kc, ia, sg, s0, norm_w)


def _head_slope(h):
    return jnp.where(h == 0, 2.0 ** -2, jnp.where(h == 1, 2.0 ** -4,
                     jnp.where(h == 2, 2.0 ** -6, 2.0 ** -8))).astype(F32)


def _attn_kernel(q_ref, k_ref, v_ref, lam_ref, nw_ref, o_ref, s_scr, m_scr, l_scr, acc_scr,
                 *, tq, lam_init):
    h = pl.program_id(1)
    i = pl.program_id(2)
    qf = q_ref[...].astype(F32)
    lane = lax.broadcasted_iota(jnp.int32, (tq, 2 * DH_B), 1)
    q2 = jnp.concatenate([jnp.where(lane < DH_B, qf, 0.0),
                          jnp.where(lane >= DH_B, qf, 0.0)], axis=0).astype(BF16)
    slope = _head_slope(h)
    k_iota = lax.broadcasted_iota(jnp.int32, (1, tq), 1)
    q_rel = lax.broadcasted_iota(jnp.int32, (2 * tq, 1), 0) % tq

    m_scr[...] = jnp.full(m_scr.shape, NEG, F32)

    def scores(j, masked):
        k = k_ref[pl.ds(pl.multiple_of(j * tq, tq), tq), :]
        s = _dot_nt(q2, k)
        s = s + slope * (k_iota + (j - i) * tq).astype(F32)
        if masked:
            s = jnp.where(k_iota <= q_rel, s, NEG)
        s_scr[j] = s
        m_scr[...] = jnp.maximum(m_scr[...], s)

    def pass1(j, carry):
        scores(j, False)
        return carry

    lax.fori_loop(0, i, pass1, 0)
    scores(i, True)
    m_scr[...] = jnp.broadcast_to(jnp.max(m_scr[...], axis=-1, keepdims=True), m_scr.shape)

    l_scr[...] = jnp.zeros(l_scr.shape, F32)
    acc_scr[...] = jnp.zeros(acc_scr.shape, F32)

    def pass2(j, carry):
        p = jnp.exp(s_scr[j] - m_scr[...])
        l_scr[...] += p
        v = v_ref[pl.ds(pl.multiple_of(j * tq, tq), tq), :]
        acc_scr[...] += _dot(p.astype(BF16), v)
        return carry

    lax.fori_loop(0, i + 1, pass2, 0)
    l = jnp.sum(l_scr[...], axis=-1, keepdims=True)
    o = _diff_norm(acc_scr[0:tq], l[0:tq], acc_scr[tq:2 * tq], l[tq:2 * tq],
                   _lambda(lam_ref, lam_init), nw_ref[...], lam_init)
    o_ref[...] = o.astype(o_ref.dtype)


def _attn_prompt(qb16, kb16, vb16, lam_vecs, norm_w, layer, lam_init):
    bsz, t, _ = qb16.shape
    tq = min(256, t)
    return pl.pallas_call(
        functools.partial(_attn_kernel, tq=tq, lam_init=lam_init),
        grid=(bsz, H_B, t // tq),
        in_specs=[pl.BlockSpec((None, tq, DV_B), lambda b, h, i: (b, i, h)),
                  pl.BlockSpec((None, t, DV_B), lambda b, h, i: (b, 0, h)),
                  pl.BlockSpec((None, t, DV_B), lambda b, h, i: (b, 0, h)),
                  pl.BlockSpec((None, 4, DH_B), lambda b, h, i: (layer, 0, 0)),
                  pl.BlockSpec((None, 1, DV_B), lambda b, h, i: (layer, 0, 0))],
        out_specs=pl.BlockSpec((None, tq, DV_B), lambda b, h, i: (b, i, h)),
        out_shape=jax.ShapeDtypeStruct((bsz, t, W_B), BF16),
        scratch_shapes=[pltpu.VMEM((t // tq, 2 * tq, tq), F32),
                        pltpu.VMEM((2 * tq, tq), F32), pltpu.VMEM((2 * tq, tq), F32),
                        pltpu.VMEM((2 * tq, DV_B), F32)],
        compiler_params=_cparams(("parallel", "parallel", "arbitrary")),
        name="diff_attn_prompt",
    )(qb16, kb16, vb16, lam_vecs, norm_w)


def _softmax_update(s, v16, m_scr, l_scr, acc_scr):
    m_prev = m_scr[...]
    m_new = jnp.maximum(m_prev, jnp.max(s, axis=-1, keepdims=True))
    alpha = jnp.exp(m_prev - m_new)
    p = jnp.exp(s - m_new)
    l_scr[...] = alpha * l_scr[...] + jnp.sum(p, axis=-1, keepdims=True)
    acc_scr[...] = alpha * acc_scr[...] + _dot(p.astype(BF16), v16)
    m_scr[...] = m_new


def _dec_kernel(pt_ref, q_ref, kn_ref, vn_ref, lam_ref, nw_ref, *rest,
                n_pg, page, past_len, lam_init, t_valid):
    del pt_ref
    k_refs = rest[:n_pg]
    v_refs = rest[n_pg:2 * n_pg]
    o_ref = rest[2 * n_pg]
    qx_scr, m_scr, l_scr, acc_scr, k16_scr, v16_scr = rest[2 * n_pg + 1:]
    j = pl.program_id(1)
    tp = SAMPLE_PAD_T
    slab = page * H_B
    n_rows = H_B * 2 * tp
    row = lax.broadcasted_iota(jnp.int32, (n_rows, 1), 0)
    row_head = row // (2 * tp)
    slope = jnp.where(row_head == 0, 2.0 ** -2, jnp.where(row_head == 1, 2.0 ** -4,
                      jnp.where(row_head == 2, 2.0 ** -6, 2.0 ** -8))).astype(F32)

    @pl.when(j == 0)
    def _():
        qf = q_ref[...]
        lane = lax.broadcasted_iota(jnp.int32, (tp, 2 * DH_B), 1)
        parts = []
        for h in range(H_B):
            qh = qf[:, h * 2 * DH_B:(h + 1) * 2 * DH_B]
            parts.append(jnp.where(lane < DH_B, qh, 0.0))
            parts.append(jnp.where(lane >= DH_B, qh, 0.0))
        qx_scr[...] = jnp.concatenate(parts, axis=0).astype(BF16)
        m_scr[...] = jnp.full(m_scr.shape, NEG, F32)
        l_scr[...] = jnp.zeros(l_scr.shape, F32)
        acc_scr[...] = jnp.zeros(acc_scr.shape, F32)

    for p in range(n_pg):
        k16_scr[p * slab:(p + 1) * slab, :] = k_refs[p][...].astype(BF16)
        v16_scr[p * slab:(p + 1) * slab, :] = v_refs[p][...].astype(BF16)
    qx = qx_scr[...]
    col = lax.broadcasted_iota(jnp.int32, (1, n_pg * slab), 1)
    s = _dot_nt(qx, k16_scr[...])
    k_pos = j * (n_pg * page) + col // H_B
    s = jnp.where(col % H_B == row_head, s + slope * (k_pos - past_len).astype(F32), NEG)
    _softmax_update(s, v16_scr[...], m_scr, l_scr, acc_scr)

    @pl.when(j == pl.num_programs(1) - 1)
    def _():
        col_n = lax.broadcasted_iota(jnp.int32, (1, slab), 1)
        t_k = col_n // H_B
        s_n = _dot_nt(qx, kn_ref[...].astype(BF16))
        ok = (col_n % H_B == row_head) & (t_k <= row % tp) & (t_k < t_valid)
        s_n = jnp.where(ok, s_n + slope * t_k.astype(F32), NEG)
        _softmax_update(s_n, vn_ref[...].astype(BF16), m_scr, l_scr, acc_scr)
        lam = _lambda(lam_ref, lam_init)
        nw = nw_ref[...]
        for h in range(H_B):
            r0 = h * 2 * tp
            o = _diff_norm(acc_scr[r0:r0 + tp], l_scr[r0:r0 + tp],
                           acc_scr[r0 + tp:r0 + 2 * tp], l_scr[r0 + tp:r0 + 2 * tp],
                           lam, nw, lam_init)
            o_ref[:, h * DV_B:(h + 1) * DV_B] = o.astype(o_ref.dtype)


def _attn_sample(qb, kn_pad, vn_pad, cache_k, cache_v, page_table, lam_vecs, norm_w,
                 layer, lam_init, t_valid):
    bsz, tp, _ = qb.shape
    n_pages = page_table.shape[1]
    slab = cache_k.shape[2]
    page = slab // H_B
    n_pg = 8 if n_pages % 8 == 0 else 1
    n_rows = H_B * 2 * tp

    def page_spec(p):
        return pl.BlockSpec((None, None, slab, DV_B),
                            lambda b, j, pt: (layer, pt[b, j * n_pg + p], 0, 0))

    grid_spec = pltpu.PrefetchScalarGridSpec(
        num_scalar_prefetch=1,
        grid=(bsz, n_pages // n_pg),
        in_specs=[pl.BlockSpec((None, tp, W_B), lambda b, j, pt: (b, 0, 0)),
                  pl.BlockSpec((None, slab, DV_B), lambda b, j, pt: (b, 0, 0)),
                  pl.BlockSpec((None, slab, DV_B), lambda b, j, pt: (b, 0, 0)),
                  pl.BlockSpec((None, 4, DH_B), lambda b, j, pt: (layer, 0, 0)),
                  pl.BlockSpec((None, 1, DV_B), lambda b, j, pt: (layer, 0, 0))]
                 + [page_spec(p) for p in range(n_pg)] * 2,
        out_specs=pl.BlockSpec((None, tp, W_B), lambda b, j, pt: (b, 0, 0)),
        scratch_shapes=[pltpu.VMEM((n_rows, 2 * DH_B), BF16),
                        pltpu.VMEM((n_rows, 1), F32), pltpu.VMEM((n_rows, 1), F32),
                        pltpu.VMEM((n_rows, DV_B), F32),
                        pltpu.VMEM((n_pg * slab, DV_B), BF16),
                        pltpu.VMEM((n_pg * slab, DV_B), BF16)],
    )
    return pl.pallas_call(
        functools.partial(_dec_kernel, n_pg=n_pg, page=page, past_len=n_pages * page,
                          lam_init=lam_init, t_valid=t_valid),
        grid_spec=grid_spec,
        out_shape=jax.ShapeDtypeStruct((bsz, tp, W_B), F32),
        compiler_params=_cparams(("parallel", "arbitrary")),
        name="diff_attn_sample",
    )(page_table, qb, kn_pad, vn_pad, lam_vecs, norm_w,
      *([cache_k] * n_pg), *([cache_v] * n_pg))


def _oproj_kernel(oa_ref, ob_ref, x_ref, w_ref, g_ref, b_ref, o_ref):
    mixed = (_dot(oa_ref[...].astype(BF16), w_ref[0:W_A, :])
             + _dot(ob_ref[...].astype(BF16), w_ref[W_A:W_A + W_B, :]))
    o_ref[...] = _layernorm(ALPHA * x_ref[...] + mixed, g_ref[...], b_ref[...])


def _oproj(oa, ob, x, w_o16, ln_g, ln_b, layer):
    n = x.shape[0]
    tm = min(512, n)
    row = lambda i: (i, 0)
    vec = pl.BlockSpec((None, 1, D_MODEL), lambda i: (layer, 0, 0))
    return pl.pallas_call(
        _oproj_kernel,
        grid=(n // tm,),
        in_specs=[pl.BlockSpec((tm, W_A), row), pl.BlockSpec((tm, W_B), row),
                  pl.BlockSpec((tm, D_MODEL), row),
                  pl.BlockSpec((None, D_MODEL, D_MODEL), lambda i: (layer, 0, 0)),
                  vec, vec],
        out_specs=pl.BlockSpec((tm, D_MODEL), row),
        out_shape=jax.ShapeDtypeStruct((n, D_MODEL), F32),
        compiler_params=_cparams(("parallel",)),
        name="oproj_ln",
    )(oa, ob, x, w_o16, ln_g, ln_b)


def _ffn_kernel(*refs, tm, seq_len, per_row_state):
    if per_row_state:
        (x_ref, wup_ref, cw_ref, cb_ref, wdn_ref, g_ref, b_ref, p1_ref, p2_ref,
         o_ref, a_ref) = refs
    else:
        (x_ref, wup_ref, cw_ref, cb_ref, wdn_ref, g_ref, b_ref,
         o_ref, cs_ref, carry_scr) = refs
        ti = pl.program_id(1)

        @pl.when(ti == 0)
        def _():
            carry_scr[...] = jnp.zeros(carry_scr.shape, F32)

    x = x_ref[...]
    x16 = x.astype(BF16)
    row = lax.broadcasted_iota(jnp.int32, (tm, 1), 0)
    y = jnp.zeros((tm, D_MODEL), F32)
    for part in range(FF_SPLIT):
        cs = slice(part * FF_PART, (part + 1) * FF_PART)
        a = _dot(x16, wup_ref[:, cs])
        g = _dot(x16, wup_ref[:, D_FF + part * FF_PART:D_FF + (part + 1) * FF_PART])
        r1 = pltpu.roll(a, 1, 0)
        r2 = pltpu.roll(a, 2, 0)
        if per_row_state:
            t = row % seq_len
            am1 = jnp.where(t >= 1, r1, p1_ref[:, cs])
            am2 = jnp.where(t >= 2, r2, p2_ref[:, cs])
            a_ref[:, cs] = a
        else:
            last = carry_scr[1:2, cs]
            prev = carry_scr[0:1, cs]
            am1 = jnp.where(row == 0, last, r1)
            am2 = jnp.where(row == 0, prev, jnp.where(row == 1, last, r2))
            carry_scr[:, cs] = a[tm - 2:tm]

            @pl.when(ti == pl.num_programs(1) - 1)
            def _():
                cs_ref[:, cs] = a[tm - 2:tm]

        cw = cw_ref[:, cs]
        c = cb_ref[:, cs] + am2 * cw[0:1] + am1 * cw[1:2] + a * cw[2:3]
        y = y + _dot((_silu(c) * g).astype(BF16), wdn_ref[cs, :])
    o_ref[...] = _layernorm(ALPHA * x + y, g_ref[...], b_ref[...])


def _ffn_specs(layer):
    return [pl.BlockSpec((None, D_MODEL, 2 * D_FF), lambda *a: (layer, 0, 0)),
            pl.BlockSpec((None, CONV_W, D_FF), lambda *a: (layer, 0, 0)),
            pl.BlockSpec((None, 1, D_FF), lambda *a: (layer, 0, 0)),
            pl.BlockSpec((None, D_FF, D_MODEL), lambda *a: (layer, 0, 0)),
            pl.BlockSpec((None, 1, D_MODEL), lambda *a: (layer, 0, 0)),
            pl.BlockSpec((None, 1, D_MODEL), lambda *a: (layer, 0, 0))]


def _ffn_prompt(x, w_up16, conv_w, conv_b, w_dn16, ln_g, ln_b, layer):
    bsz, t, _ = x.shape
    tm = min(256, t)
    return pl.pallas_call(
        functools.partial(_ffn_kernel, tm=tm, seq_len=t, per_row_state=False),
        grid=(bsz, t // tm),
        in_specs=[pl.BlockSpec((None, tm, D_MODEL), lambda b, i: (b, i, 0))] + _ffn_specs(layer),
        out_specs=[pl.BlockSpec((None, tm, D_MODEL), lambda b, i: (b, i, 0)),
                   pl.BlockSpec((None, CONV_W - 1, D_FF), lambda b, i: (b, 0, 0))],
        out_shape=[jax.ShapeDtypeStruct((bsz, t, D_MODEL), F32),
                   jax.ShapeDtypeStruct((bsz, CONV_W - 1, D_FF), F32)],
        scratch_shapes=[pltpu.VMEM((CONV_W - 1, D_FF), F32)],
        compiler_params=_cparams(("parallel", "arbitrary")),
        name="convffn_prompt",
    )(x, w_up16, conv_w, conv_b, w_dn16, ln_g, ln_b)


def _ffn_sample(x, w_up16, conv_w, conv_b, w_dn16, ln_g, ln_b, p1, p2, layer, seq_len):
    n = x.shape[0]
    full = lambda i: (0, 0)
    return pl.pallas_call(
        functools.partial(_ffn_kernel, tm=n, seq_len=seq_len, per_row_state=True),
        grid=(1,),
        in_specs=[pl.BlockSpec((n, D_MODEL), full)] + _ffn_specs(layer)
                 + [pl.BlockSpec((n, D_FF), full), pl.BlockSpec((n, D_FF), full)],
        out_specs=[pl.BlockSpec((n, D_MODEL), full), pl.BlockSpec((n, D_FF), full)],
        out_shape=[jax.ShapeDtypeStruct((n, D_MODEL), F32),
                   jax.ShapeDtypeStruct((n, D_FF), F32)],
        compiler_params=_cparams(("arbitrary",)),
        name="convffn_sample",
    )(x, w_up16, conv_w, conv_b, w_dn16, ln_g, ln_b, p1, p2)


def kernel(x_prompt, x_sample, cache_k, cache_v, state_hgrn, state_ffn_conv, page_table,
           ln_in_g, ln_in_b, w_in, hgrn_lb_logits, hgrn_norm_w,
           lambda_q1, lambda_k1, lambda_q2, lambda_k2, diff_norm_w, w_o,
           ln1_g, ln1_b, w_up, conv_w, conv_b, w_down, ln2_g, ln2_b):
    bp, tp_, _ = x_prompt.shape
    bs, ts, _ = x_sample.shape
    pad_t = SAMPLE_PAD_T
    n_phys, page = cache_k.shape[1], cache_k.shape[2]
    slab = page * H_B
    np_rows, ns_rows = bp * tp_, bs * pad_t

    w_in16 = w_in.astype(BF16)
    w_o16 = w_o.astype(BF16)
    w_up16 = w_up.astype(BF16)
    w_dn16 = w_down.astype(BF16)
    lam_vecs = jnp.stack([lambda_q1, lambda_k1, lambda_q2, lambda_k2], axis=1)
    ck = cache_k.reshape(DEPTH, n_phys, slab, 2 * DH_B)
    cv = cache_v.reshape(DEPTH, n_phys, slab, DV_B)
    s_zero = jnp.zeros((bp, H_A, DK_A, DV_A), F32)
    per_layer = lambda a: a.reshape(DEPTH, 1, a.shape[-1])
    hgrn_norm_w, diff_norm_w, conv_b = per_layer(hgrn_norm_w), per_layer(diff_norm_w), per_layer(conv_b)
    ln1_g, ln1_b, ln2_g, ln2_b = per_layer(ln1_g), per_layer(ln1_b), per_layer(ln2_g), per_layer(ln2_b)

    xp = _ln_in(x_prompt.reshape(np_rows, D_MODEL), ln_in_g, ln_in_b)
    xs = _ln_in(jnp.pad(x_sample, ((0, 0), (0, pad_t - ts), (0, 0))).reshape(ns_rows, D_MODEL),
                ln_in_g, ln_in_b)

    k_p, v_p, k_s, v_s, s_p, s_s, c_p, c_s = [], [], [], [], [], [], [], []
    for l in range(DEPTH):
        lam_init = 0.8 - 0.6 * math.exp(-0.3 * l)

        qa, logf, kc, ia, sg, qb, kb, vb, kb16, vb16 = _inproj(xp, w_in16, hgrn_lb_logits, l, BF16)
        seq = lambda a: a.reshape(bp, tp_, a.shape[-1])
        oa, sp = _hgrn(seq(qa), seq(logf), seq(kc), seq(ia), seq(sg), s_zero, hgrn_norm_w, l, tp_)
        ob = _attn_prompt(seq(qb), seq(kb16), seq(vb16), lam_vecs, diff_norm_w, l, lam_init)
        x1 = _oproj(oa.reshape(np_rows, W_A), ob.reshape(np_rows, W_B), xp, w_o16, ln1_g, ln1_b, l)
        x2, cp = _ffn_prompt(seq(x1), w_up16, conv_w, conv_b, w_dn16, ln2_g, ln2_b, l)
        xp = x2.reshape(np_rows, D_MODEL)
        k_p.append(kb.reshape(bp, tp_, H_B, 2 * DH_B))
        v_p.append(vb.reshape(bp, tp_, H_B, DV_B))
        s_p.append(sp)
        c_p.append(cp)

        qa, logf, kc, ia, sg, qb, kb, vb, _, _ = _inproj(xs, w_in16, hgrn_lb_logits, l, F32)
        seq = lambda a: a.reshape(bs, pad_t, a.shape[-1])
        oa, ss = _hgrn(seq(qa), seq(logf), seq(kc), seq(ia), seq(sg), state_hgrn[l],
                       hgrn_norm_w, l, ts)
        grow = lambda a: jnp.pad(a.reshape(bs, pad_t * H_B, DV_B),
                                 ((0, 0), (0, slab - pad_t * H_B), (0, 0)))
        ob = _attn_sample(seq(qb), grow(kb), grow(vb), ck, cv, page_table, lam_vecs,
                          diff_norm_w, l, lam_init, ts)
        x1 = _oproj(oa.reshape(ns_rows, W_A), ob.reshape(ns_rows, W_B), xs, w_o16, ln1_g, ln1_b, l)
        conv0 = state_ffn_conv[l]
        p1 = jnp.pad(conv0[:, 1:2], ((0, 0), (0, pad_t - 1), (0, 0))).reshape(ns_rows, D_FF)
        p2 = jnp.pad(conv0, ((0, 0), (0, pad_t - 2), (0, 0))).reshape(ns_rows, D_FF)
        xs, a_s = _ffn_sample(x1, w_up16, conv_w, conv_b, w_dn16, ln2_g, ln2_b, p1, p2, l, pad_t)
        k_s.append(seq(kb)[:, :ts].reshape(bs, ts, H_B, 2 * DH_B))
        v_s.append(seq(vb)[:, :ts].reshape(bs, ts, H_B, DV_B))
        s_s.append(ss)
        c_s.append(a_s.reshape(bs, pad_t, D_FF)[:, ts - (CONV_W - 1):ts])

    y_p = xp.reshape(bp, tp_, D_MODEL)
    y_s = xs.reshape(bs, pad_t, D_MODEL)[:, :ts]
    return (y_p, y_s, jnp.stack(k_p), jnp.stack(v_p), jnp.stack(k_s), jnp.stack(v_s),
            jnp.stack(s_p), jnp.stack(s_s), jnp.stack(c_p), jnp.stack(c_s))
```

```python
import functools
import math

import jax
import jax.numpy as jnp
from jax import lax
from jax.experimental import pallas as pl
from jax.experimental.pallas import tpu as pltpu

F32 = jnp.float32
BF16 = jnp.bfloat16

D_MODEL = 1024
DEPTH = 4
H_A, DK_A, DV_A = 4, 128, 128
H_B, DH_B, DV_B = 4, 64, 128
W_A = H_A * DV_A
W_B = H_B * DV_B
IN_COLS = 2 * H_A * DK_A + 2 * W_A + 4 * H_B * DH_B + W_B
D_FF = 2816
FF_CHUNK = 256
FF_ROWS = 32
CONV_W = 3
CHUNK = 128
ALPHA = (2 * DEPTH) ** 0.25
LN_EPS = 1e-5
RMS_EPS = 1e-6
LOG2_E = 1.0 / math.log(2.0)
NEG = -1e30
LANES = 128
SAMPLE_PAD_T = 8
DEC_PAGES_PER_STEP = 16
VMEM_LIMIT = 56 * 1024 * 1024


def _cparams(sem):
    return pltpu.CompilerParams(dimension_semantics=sem, vmem_limit_bytes=VMEM_LIMIT)


def _layernorm(x, g, b):
    mu = jnp.mean(x, axis=-1, keepdims=True)
    xc = x - mu
    var = jnp.mean(xc * xc, axis=-1, keepdims=True)
    return xc * lax.rsqrt(var + LN_EPS) * g + b


def _silu(x):
    return x / (1.0 + jnp.exp(-x))


def _dot(a, b):
    return jnp.dot(a, b, preferred_element_type=F32)


def _dot_nt(a, b):
    return lax.dot_general(a, b, (((1,), (1,)), ((), ())), preferred_element_type=F32)


def _dot_tn(a, b):
    return lax.dot_general(a, b, (((0,), (0,)), ((), ())), preferred_element_type=F32)


def _lambda(lam_ref, lam_init):
    lv = lam_ref[...]
    p1 = jnp.sum(lv[0:1] * lv[1:2], axis=-1, keepdims=True)
    p2 = jnp.sum(lv[2:3] * lv[3:4], axis=-1, keepdims=True)
    return jnp.exp(p1) - jnp.exp(p2) + lam_init


def _diff_norm(acc1, l1, acc2, l2, lam, nw, lam_init):
    o = acc1 / l1 - lam * (acc2 / l2)
    ms = jnp.mean(o * o, axis=-1, keepdims=True)
    return o * lax.rsqrt(ms + RMS_EPS) * nw * (1.0 - lam_init)


def _ln_kernel(x_ref, g_ref, b_ref, o_ref):
    o_ref[...] = _layernorm(x_ref[...], g_ref[...], b_ref[...])


def _ln_in(x, g, b):
    n = x.shape[0]
    tm = min(512, n)
    return pl.pallas_call(
        _ln_kernel,
        grid=(n // tm,),
        in_specs=[pl.BlockSpec((tm, D_MODEL), lambda i: (i, 0)),
                  pl.BlockSpec((1, D_MODEL), lambda i: (0, 0)),
                  pl.BlockSpec((1, D_MODEL), lambda i: (0, 0))],
        out_specs=pl.BlockSpec((tm, D_MODEL), lambda i: (i, 0)),
        out_shape=jax.ShapeDtypeStruct((n, D_MODEL), F32),
        compiler_params=_cparams(("parallel",)),
        name="ln_in",
    )(x, g.reshape(1, D_MODEL), b.reshape(1, D_MODEL))


def _inproj_kernel(*refs, layer, flat_kv):
    if flat_kv:
        if layer == 0:
            x_ref, w_ref, lbl_ref = refs[:3]
        else:
            x_ref, w_ref, lbl_ref, _, _ = refs[:5]
        (qa_ref, logf_ref, kc_ref, ia_ref, sg_ref, qb_ref,
         kb16_ref, vb16_ref, kflat_ref, vflat_ref) = refs[-10:]
    else:
        (x_ref, w_ref, lbl_ref, qa_ref, logf_ref, kc_ref, ia_ref, sg_ref,
         qb_ref, kb_ref, vb_ref) = refs
    x = x_ref[...].astype(BF16)
    tm = x.shape[0]

    def proj(c0, n):
        return _dot(x, w_ref[:, c0:c0 + n])

    qa = proj(0, W_A)
    qa_ref[...] = _silu(qa) * (DK_A ** -0.5)

    fa = proj(W_A, W_A)
    log_sig = jnp.minimum(fa, 0.0) - jnp.log1p(jnp.exp(-jnp.abs(fa)))
    sig_neg = 1.0 / (1.0 + jnp.exp(fa))
    if layer == 0:
        logf_ref[...] = log_sig
        kc_ref[...] = sig_neg
    else:
        lg = lbl_ref[...]
        e = jnp.exp(lg - jnp.max(lg, axis=0, keepdims=True))
        sm = e / jnp.sum(e, axis=0, keepdims=True)
        lb = sm[1:2]
        for j in range(2, layer + 1):
            lb = lb + sm[j:j + 1]
        u = jnp.log(lb)
        w = jnp.log1p(-lb) + log_sig
        logf_ref[...] = jnp.maximum(u, w) + jnp.log1p(jnp.exp(-jnp.abs(u - w)))
        kc_ref[...] = (1.0 - lb) * sig_neg

    ia_ref[...] = proj(2 * W_A, W_A)
    sg_ref[...] = _silu(proj(3 * W_A, W_A))
    qb_ref[...] = (proj(4 * W_A, W_B) * (DH_B ** -0.5)).astype(qb_ref.dtype)
    kb = proj(4 * W_A + W_B, W_B)
    vb = proj(4 * W_A + 2 * W_B, W_B)
    if flat_kv:
        kb16_ref[...] = kb.astype(BF16)
        vb16_ref[...] = vb.astype(BF16)
        for h in range(H_B):
            hs = slice(h * DV_B, (h + 1) * DV_B)
            kflat_ref[pl.ds(h, tm, stride=H_B), :] = kb[:, hs]
            vflat_ref[pl.ds(h, tm, stride=H_B), :] = vb[:, hs]
    else:
        kb_ref[...] = kb
        vb_ref[...] = vb


def _inproj(x, w_in16, lb_logits, layer, kv_flat=None):
    n = x.shape[0]
    tm = min(512, n)
    row = lambda i: (i, 0)
    f32_out = jax.ShapeDtypeStruct((n, W_A), F32)
    bf_out = jax.ShapeDtypeStruct((n, W_A), BF16)
    blk = pl.BlockSpec((tm, W_A), row)
    in_specs = [pl.BlockSpec((tm, D_MODEL), row),
                pl.BlockSpec((None, D_MODEL, IN_COLS), lambda i: (layer, 0, 0),
                             pipeline_mode=pl.Buffered(1)),
                pl.BlockSpec((DEPTH, W_A), lambda i: (0, 0))]
    if kv_flat is None:
        out_specs = [blk] * 8
        out_shape = [f32_out] * 8
        args, aliases = (), {}
    else:
        flat = jax.ShapeDtypeStruct((DEPTH, n * H_B, DV_B), F32)
        out_specs = [blk] * 8 + [pl.BlockSpec((None, tm * H_B, DV_B), lambda i: (layer, i, 0))] * 2
        out_shape = [f32_out] * 5 + [bf_out] * 3 + [flat, flat]
        args = tuple(kv_flat)
        in_specs += [pl.BlockSpec(memory_space=pl.ANY)] * len(args)
        aliases = {3: 8, 4: 9} if args else {}
    return pl.pallas_call(
        functools.partial(_inproj_kernel, layer=layer, flat_kv=kv_flat is not None),
        grid=(n // tm,),
        in_specs=in_specs,
        out_specs=out_specs,
        out_shape=out_shape,
        input_output_aliases=aliases,
        compiler_params=_cparams(("parallel",)),
        name="inproj",
    )(x, w_in16, lb_logits, *args)


def _boundary_rows(b, m, rows):
    c, w = b.shape
    if 2 * m >= 8:
        parts = [jnp.broadcast_to(b[p * 2 * m + m - 1:p * 2 * m + m], (2 * m, w))
                 for p in range(c // (2 * m))]
        return parts[0] if len(parts) == 1 else jnp.concatenate(parts, axis=0)
    if m == 2:
        r = rows % 4
        return jnp.where(r == 0, pltpu.roll(b, c - 1, 0),
                         jnp.where(r == 1, b,
                                   jnp.where(r == 2, pltpu.roll(b, 1, 0), pltpu.roll(b, 2, 0))))
    return jnp.where(rows % 2 == 0, b, pltpu.roll(b, 1, 0))


def _hgrn_kernel(q_ref, g_ref, kc_ref, v_ref, sg_ref, s0_ref, nw_ref, o_ref, s_ref, st_scr,
                 *, chunk, t_valid, t_total):
    ci = pl.program_id(1)

    @pl.when(ci == 0)
    def _():
        st_scr[...] = jnp.concatenate([s0_ref[h].T for h in range(H_A)], axis=1)

    rows = lax.broadcasted_iota(jnp.int32, (chunk, 1), 0)
    t_idx = lax.broadcasted_iota(jnp.int32, (chunk, chunk), 0)
    s_idx = lax.broadcasted_iota(jnp.int32, (chunk, chunk), 1)
    tri = (s_idx <= t_idx).astype(F32)

    q = q_ref[...]
    g = g_ref[...]
    kc = kc_ref[...]
    if t_valid < t_total:
        valid = (ci * chunk + rows) < t_valid
        g = jnp.where(valid, g, 0.0)
        kc = jnp.where(valid, kc, 0.0)
    v16 = v_ref[...].astype(BF16)
    heads = [slice(h * DK_A, (h + 1) * DK_A) for h in range(H_A)]

    b = jnp.dot(tri, g, preferred_element_type=F32, precision=lax.Precision.HIGHEST) * LOG2_E

    level = jnp.where(s_idx <= t_idx, 31 - lax.clz(t_idx ^ s_idx), -2)
    q16 = q.astype(BF16)
    k16 = kc.astype(BF16)
    a = [jnp.where(level == -1, _dot_nt(q16[:, hs], k16[:, hs]), 0.0) for hs in heads]
    m = chunk // 2
    while m >= 1:
        r = _boundary_rows(b, m, rows)
        second_half = (rows // m) % 2 == 1
        x = (jnp.where(second_half, q, kc) * jnp.exp2(-jnp.abs(b - r))).astype(BF16)
        keep = level == (m.bit_length() - 1)
        a = [jnp.where(keep, _dot_nt(x[:, hs], x[:, hs]), a[h]) for h, hs in enumerate(heads)]
        m //= 2

    qe = (q * jnp.exp2(b)).astype(BF16)
    b_last = b[chunk - 1:chunk]
    decay = jnp.exp2(b_last)
    k_last = (kc * jnp.exp2(b_last - b)).astype(BF16)
    st = st_scr[...]
    st16 = st.astype(BF16)
    o = [_dot_nt(qe[:, hs], st16[:, hs]) + _dot(a[h].astype(BF16), v16[:, hs])
         for h, hs in enumerate(heads)]
    st_new = st * decay + jnp.concatenate(
        [_dot_tn(v16[:, hs], k_last[:, hs]) for hs in heads], axis=1)
    st_scr[...] = st_new
    nw = nw_ref[...]
    scale = jnp.concatenate(
        [jnp.broadcast_to(lax.rsqrt(jnp.mean(oh * oh, axis=-1, keepdims=True) + RMS_EPS),
                          (chunk, DV_A)) * nw for oh in o], axis=1)
    o_ref[...] = (jnp.concatenate(o, axis=1) * scale * sg_ref[...]).astype(o_ref.dtype)

    @pl.when(ci == pl.num_programs(1) - 1)
    def _():
        for h, hs in enumerate(heads):
            s_ref[h] = st_new[:, hs].T


def _hgrn(qa, logf, kc, ia, sg, s0, norm_w, layer, t_valid):
    bsz, t, _ = qa.shape
    chunk = CHUNK if t % CHUNK == 0 else t
    seq = pl.BlockSpec((None, chunk, W_A), lambda b, c: (b, c, 0))
    state = pl.BlockSpec((None, H_A, DK_A, DV_A), lambda b, c: (b, 0, 0, 0))
    return pl.pallas_call(
        functools.partial(_hgrn_kernel, chunk=chunk, t_valid=t_valid, t_total=t),
        grid=(bsz, t // chunk),
        in_specs=[seq, seq, seq, seq, seq, state,
                  pl.BlockSpec((None, 1, DV_A), lambda b, c: (layer, 0, 0))],
        out_specs=[seq, state],
        out_shape=[jax.ShapeDtypeStruct((bsz, t, W_A), BF16),
                   jax.ShapeDtypeStruct((bsz, H_A, DK_A, DV_A), F32)],
        scratch_shapes=[pltpu.VMEM((DV_A, H_A * DK_A), F32)],
        compiler_params=_cparams(("parallel", "arbitrary")),
        name="hgrn2",
    )(qa, logf, kc, ia, sg, s0, norm_w)


def _head_slope(h):
    return jnp.where(h == 0, 2.0 ** -2, jnp.where(h == 1, 2.0 ** -4,
                     jnp.where(h == 2, 2.0 ** -6, 2.0 ** -8))).astype(F32)


def _fold_lanes(x, op):
    out = x[:, 0:LANES]
    for c in range(LANES, x.shape[1], LANES):
        out = op(out, x[:, c:c + LANES])
    return out


def _attn_kernel(q_ref, k_ref, v_ref, lam_ref, nw_ref, o_ref, *, tq, n_q, lam_init):
    h = pl.program_id(1)
    slope = _head_slope(h)
    lam = _lambda(lam_ref, lam_init)
    nw = nw_ref[...]
    lane = lax.broadcasted_iota(jnp.int32, (tq, 2 * DH_B), 1)
    k_diag = lax.broadcasted_iota(jnp.int32, (1, tq), 1)
    causal = k_diag <= lax.broadcasted_iota(jnp.int32, (2 * tq, 1), 0) % tq
    for i in range(n_q):
        qf = q_ref[i * tq:(i + 1) * tq, :].astype(F32)
        q2 = jnp.concatenate([jnp.where(lane < DH_B, qf, 0.0),
                              jnp.where(lane >= DH_B, qf, 0.0)], axis=0).astype(BF16)
        s_d = _dot_nt(q2, k_ref[i * tq:(i + 1) * tq, :]) + slope * k_diag.astype(F32)
        s_d = jnp.where(causal, s_d, NEG)
        m = _fold_lanes(s_d, jnp.maximum)
        if i > 0:
            k_off = lax.broadcasted_iota(jnp.int32, (1, i * tq), 1) - i * tq
            s_o = _dot_nt(q2, k_ref[0:i * tq, :]) + slope * k_off.astype(F32)
            m = jnp.maximum(m, _fold_lanes(s_o, jnp.maximum))
        m = jnp.max(m, axis=-1, keepdims=True)
        p_d = jnp.exp(s_d - m)
        l = _fold_lanes(p_d, jnp.add)
        acc = _dot(p_d.astype(BF16), v_ref[i * tq:(i + 1) * tq, :])
        if i > 0:
            p_o = jnp.exp(s_o - m)
            l = l + _fold_lanes(p_o, jnp.add)
            acc = acc + _dot(p_o.astype(BF16), v_ref[0:i * tq, :])
        l = jnp.sum(l, axis=-1, keepdims=True)
        o = _diff_norm(acc[0:tq], l[0:tq], acc[tq:2 * tq], l[tq:2 * tq], lam, nw, lam_init)
        o_ref[i * tq:(i + 1) * tq, :] = o.astype(o_ref.dtype)


def _attn_prompt(qb16, kb16, vb16, lam_vecs, norm_w, layer, lam_init):
    bsz, t, _ = qb16.shape
    tq = min(256, t)
    seq_head = pl.BlockSpec((None, t, DV_B), lambda b, h: (b, 0, h))
    return pl.pallas_call(
        functools.partial(_attn_kernel, tq=tq, n_q=t // tq, lam_init=lam_init),
        grid=(bsz, H_B),
        in_specs=[seq_head, seq_head, seq_head,
                  pl.BlockSpec((None, 4, DH_B), lambda b, h: (layer, 0, 0)),
                  pl.BlockSpec((None, 1, DV_B), lambda b, h: (layer, 0, 0))],
        out_specs=seq_head,
        out_shape=jax.ShapeDtypeStruct((bsz, t, W_B), BF16),
        compiler_params=_cparams(("parallel", "parallel")),
        name="diff_attn_prompt",
    )(qb16, kb16, vb16, lam_vecs, norm_w)


def _softmax_update(s, v16, m_scr, l_scr, acc_scr):
    m_prev = m_scr[...]
    m_new = jnp.maximum(m_prev, jnp.max(s, axis=-1, keepdims=True))
    alpha = jnp.exp(m_prev - m_new)
    p = jnp.exp(s - m_new)
    l_scr[...] = alpha * l_scr[...] + jnp.sum(p, axis=-1, keepdims=True)
    acc_scr[...] = alpha * acc_scr[...] + _dot(p.astype(BF16), v16)
    m_scr[...] = m_new


def _dec_kernel(pt_ref, q_ref, kn_ref, vn_ref, lam_ref, nw_ref, *rest,
                n_pg, page, past_len, lam_init, t_valid):
    del pt_ref
    k_refs = rest[:n_pg]
    v_refs = rest[n_pg:2 * n_pg]
    o_ref = rest[2 * n_pg]
    qx_scr, m_scr, l_scr, acc_scr = rest[2 * n_pg + 1:]
    j = pl.program_id(1)
    tp = SAMPLE_PAD_T
    slab = page * H_B
    n_rows = H_B * 2 * tp
    row = lax.broadcasted_iota(jnp.int32, (n_rows, 1), 0)
    row_head = row // (2 * tp)
    slope = jnp.where(row_head == 0, 2.0 ** -2, jnp.where(row_head == 1, 2.0 ** -4,
                      jnp.where(row_head == 2, 2.0 ** -6, 2.0 ** -8))).astype(F32)

    @pl.when(j == 0)
    def _():
        qf = q_ref[...]
        lane = lax.broadcasted_iota(jnp.int32, (tp, 2 * DH_B), 1)
        parts = []
        for h in range(H_B):
            qh = qf[:, h * 2 * DH_B:(h + 1) * 2 * DH_B]
            parts.append(jnp.where(lane < DH_B, qh, 0.0))
            parts.append(jnp.where(lane >= DH_B, qh, 0.0))
        qx_scr[...] = jnp.concatenate(parts, axis=0).astype(BF16)
        m_scr[...] = jnp.full(m_scr.shape, NEG, F32)
        l_scr[...] = jnp.zeros(l_scr.shape, F32)
        acc_scr[...] = jnp.zeros(acc_scr.shape, F32)

    qx = qx_scr[...]
    col = lax.broadcasted_iota(jnp.int32, (1, slab), 1)
    own_head = col % H_B == row_head
    pos0 = col // H_B + (j * (n_pg * page) - past_len)
    s = [jnp.where(own_head,
                   _dot_nt(qx, k_refs[p][...].astype(BF16)) + slope * (pos0 + p * page).astype(F32),
                   NEG) for p in range(n_pg)]
    s_max = s[0]
    for sp in s[1:]:
        s_max = jnp.maximum(s_max, sp)
    m_prev = m_scr[...]
    m_new = jnp.maximum(m_prev, jnp.max(s_max, axis=-1, keepdims=True))
    alpha = jnp.exp(m_prev - m_new)
    pr = [jnp.exp(sp - m_new) for sp in s]
    p_sum = pr[0]
    for pp in pr[1:]:
        p_sum = p_sum + pp
    pv = _dot(pr[0].astype(BF16), v_refs[0][...].astype(BF16))
    for p in range(1, n_pg):
        pv = pv + _dot(pr[p].astype(BF16), v_refs[p][...].astype(BF16))
    l_scr[...] = alpha * l_scr[...] + jnp.sum(p_sum, axis=-1, keepdims=True)
    acc_scr[...] = alpha * acc_scr[...] + pv
    m_scr[...] = m_new

    @pl.when(j == pl.num_programs(1) - 1)
    def _():
        col_n = lax.broadcasted_iota(jnp.int32, (1, slab), 1)
        t_k = col_n // H_B
        s_n = _dot_nt(qx, kn_ref[...].astype(BF16))
        ok = (col_n % H_B == row_head) & (t_k <= row % tp) & (t_k < t_valid)
        s_n = jnp.where(ok, s_n + slope * t_k.astype(F32), NEG)
        _softmax_update(s_n, vn_ref[...].astype(BF16), m_scr, l_scr, acc_scr)
        lam = _lambda(lam_ref, lam_init)
        nw = nw_ref[...]
        for h in range(H_B):
            r0 = h * 2 * tp
            o = _diff_norm(acc_scr[r0:r0 + tp], l_scr[r0:r0 + tp],
                           acc_scr[r0 + tp:r0 + 2 * tp], l_scr[r0 + tp:r0 + 2 * tp],
                           lam, nw, lam_init)
            o_ref[:, h * DV_B:(h + 1) * DV_B] = o.astype(o_ref.dtype)


def _attn_sample(qb, kn_pad, vn_pad, cache_k, cache_v, page_table, lam_vecs, norm_w,
                 layer, lam_init, t_valid):
    bsz, tp, _ = qb.shape
    n_pages = page_table.shape[1]
    slab = cache_k.shape[2]
    page = slab // H_B
    n_pg = math.gcd(n_pages, DEC_PAGES_PER_STEP)
    n_rows = H_B * 2 * tp

    def page_spec(p):
        return pl.BlockSpec((None, None, slab, DV_B),
                            lambda b, j, pt: (layer, pt[b, j * n_pg + p], 0, 0))

    grid_spec = pltpu.PrefetchScalarGridSpec(
        num_scalar_prefetch=1,
        grid=(bsz, n_pages // n_pg),
        in_specs=[pl.BlockSpec((None, tp, W_B), lambda b, j, pt: (b, 0, 0)),
                  pl.BlockSpec((None, slab, DV_B), lambda b, j, pt: (b, 0, 0)),
                  pl.BlockSpec((None, slab, DV_B), lambda b, j, pt: (b, 0, 0)),
                  pl.BlockSpec((None, 4, DH_B), lambda b, j, pt: (layer, 0, 0)),
                  pl.BlockSpec((None, 1, DV_B), lambda b, j, pt: (layer, 0, 0))]
                 + [page_spec(p) for p in range(n_pg)] * 2,
        out_specs=pl.BlockSpec((None, tp, W_B), lambda b, j, pt: (b, 0, 0)),
        scratch_shapes=[pltpu.VMEM((n_rows, 2 * DH_B), BF16),
                        pltpu.VMEM((n_rows, 1), F32), pltpu.VMEM((n_rows, 1), F32),
                        pltpu.VMEM((n_rows, DV_B), F32)],
    )
    return pl.pallas_call(
        functools.partial(_dec_kernel, n_pg=n_pg, page=page, past_len=n_pages * page,
                          lam_init=lam_init, t_valid=t_valid),
        grid_spec=grid_spec,
        out_shape=jax.ShapeDtypeStruct((bsz, tp, W_B), F32),
        compiler_params=_cparams(("parallel", "arbitrary")),
        name="diff_attn_sample",
    )(page_table, qb, kn_pad, vn_pad, lam_vecs, norm_w,
      *([cache_k] * n_pg), *([cache_v] * n_pg))


def _oproj_kernel(oa_ref, ob_ref, x_ref, w_ref, g_ref, b_ref, o_ref):
    mixed = (_dot(oa_ref[...].astype(BF16), w_ref[0:W_A, :])
             + _dot(ob_ref[...].astype(BF16), w_ref[W_A:W_A + W_B, :]))
    o_ref[...] = _layernorm(ALPHA * x_ref[...] + mixed, g_ref[...], b_ref[...])


def _oproj(oa, ob, x, w_o16, ln_g, ln_b, layer):
    n = x.shape[0]
    tm = min(512, n)
    row = lambda i: (i, 0)
    vec = pl.BlockSpec((None, 1, D_MODEL), lambda i: (layer, 0, 0))
    return pl.pallas_call(
        _oproj_kernel,
        grid=(n // tm,),
        in_specs=[pl.BlockSpec((tm, W_A), row), pl.BlockSpec((tm, W_B), row),
                  pl.BlockSpec((tm, D_MODEL), row),
                  pl.BlockSpec((None, D_MODEL, D_MODEL), lambda i: (layer, 0, 0),
                               pipeline_mode=pl.Buffered(1)),
                  vec, vec],
        out_specs=pl.BlockSpec((tm, D_MODEL), row),
        out_shape=jax.ShapeDtypeStruct((n, D_MODEL), F32),
        compiler_params=_cparams(("parallel",)),
        name="oproj_ln",
    )(oa, ob, x, w_o16, ln_g, ln_b)


def _ffn_kernel(*refs, tm, seq_len, per_row_state):
    if per_row_state:
        (x_ref, wup_ref, cw_ref, cb_ref, wdn_ref, g_ref, b_ref, p1_ref, p2_ref,
         o_ref, a_ref, h_scr) = refs
    else:
        (x_ref, wup_ref, cw_ref, cb_ref, wdn_ref, g_ref, b_ref,
         o_ref, cs_ref, h_scr, a_scr, g_scr) = refs
        ti = pl.program_id(1)

        @pl.when(ti == 0)
        def _():
            a_scr[0:8, :] = jnp.zeros((8, D_FF), F32)

    x = x_ref[...]
    x16 = x.astype(BF16)
    for c0 in range(0, D_FF, FF_CHUNK):
        cs = slice(c0, c0 + FF_CHUNK)
        a = _dot(x16, wup_ref[:, cs])
        g = _dot(x16, wup_ref[:, D_FF + c0:D_FF + c0 + FF_CHUNK])
        cw = cw_ref[:, cs]
        cb = cb_ref[:, cs]
        if per_row_state:
            t = lax.broadcasted_iota(jnp.int32, (tm, 1), 0) % seq_len
            am1 = jnp.where(t >= 1, pltpu.roll(a, 1, 0), p1_ref[:, cs])
            am2 = jnp.where(t >= 2, pltpu.roll(a, 2, 0), p2_ref[:, cs])
            a_ref[:, cs] = a
            c = cb + am2 * cw[0:1] + am1 * cw[1:2] + a * cw[2:3]
            h_scr[:, cs] = (_silu(c) * g).astype(BF16)
        else:
            a_scr[8:8 + tm, cs] = a
            g_scr[:, cs] = g
            for r0 in range(0, tm, FF_ROWS):
                c = (cb + a_scr[6 + r0:6 + r0 + FF_ROWS, cs] * cw[0:1]
                     + a_scr[7 + r0:7 + r0 + FF_ROWS, cs] * cw[1:2]
                     + a_scr[8 + r0:8 + r0 + FF_ROWS, cs] * cw[2:3])
                h_scr[r0:r0 + FF_ROWS, cs] = (
                    _silu(c) * g_scr[r0:r0 + FF_ROWS, cs]).astype(BF16)
    if not per_row_state:
        @pl.when(ti == pl.num_programs(1) - 1)
        def _():
            cs_ref[...] = a_scr[6 + tm:8 + tm, :]

        a_scr[0:8, :] = a_scr[tm:tm + 8, :]
    y = _dot(h_scr[...], wdn_ref[...])
    o_ref[...] = _layernorm(ALPHA * x + y, g_ref[...], b_ref[...])


def _ffn_specs(layer):
    once = pl.Buffered(1)
    return [pl.BlockSpec((None, D_MODEL, 2 * D_FF), lambda *a: (layer, 0, 0), pipeline_mode=once),
            pl.BlockSpec((None, CONV_W, D_FF), lambda *a: (layer, 0, 0)),
            pl.BlockSpec((None, 1, D_FF), lambda *a: (layer, 0, 0)),
            pl.BlockSpec((None, D_FF, D_MODEL), lambda *a: (layer, 0, 0), pipeline_mode=once),
            pl.BlockSpec((None, 1, D_MODEL), lambda *a: (layer, 0, 0)),
            pl.BlockSpec((None, 1, D_MODEL), lambda *a: (layer, 0, 0))]


def _ffn_prompt(x, w_up16, conv_w, conv_b, w_dn16, ln_g, ln_b, layer):
    bsz, t, _ = x.shape
    tm = min(512, t)
    return pl.pallas_call(
        functools.partial(_ffn_kernel, tm=tm, seq_len=t, per_row_state=False),
        grid=(bsz, t // tm),
        in_specs=[pl.BlockSpec((None, tm, D_MODEL), lambda b, i: (b, i, 0))] + _ffn_specs(layer),
        out_specs=[pl.BlockSpec((None, tm, D_MODEL), lambda b, i: (b, i, 0)),
                   pl.BlockSpec((None, CONV_W - 1, D_FF), lambda b, i: (b, 0, 0))],
        out_shape=[jax.ShapeDtypeStruct((bsz, t, D_MODEL), F32),
                   jax.ShapeDtypeStruct((bsz, CONV_W - 1, D_FF), F32)],
        scratch_shapes=[pltpu.VMEM((tm, D_FF), BF16), pltpu.VMEM((8 + tm, D_FF), F32),
                        pltpu.VMEM((tm, D_FF), F32)],
        compiler_params=_cparams(("parallel", "arbitrary")),
        name="convffn_prompt",
    )(x, w_up16, conv_w, conv_b, w_dn16, ln_g, ln_b)


def _ffn_sample(x, w_up16, conv_w, conv_b, w_dn16, ln_g, ln_b, p1, p2, layer, seq_len):
    n = x.shape[0]
    full = lambda i: (0, 0)
    return pl.pallas_call(
        functools.partial(_ffn_kernel, tm=n, seq_len=seq_len, per_row_state=True),
        grid=(1,),
        in_specs=[pl.BlockSpec((n, D_MODEL), full)] + _ffn_specs(layer)
                 + [pl.BlockSpec((n, D_FF), full), pl.BlockSpec((n, D_FF), full)],
        out_specs=[pl.BlockSpec((n, D_MODEL), full), pl.BlockSpec((n, D_FF), full)],
        out_shape=[jax.ShapeDtypeStruct((n, D_MODEL), F32),
                   jax.ShapeDtypeStruct((n, D_FF), F32)],
        scratch_shapes=[pltpu.VMEM((n, D_FF), BF16)],
        compiler_params=_cparams(("arbitrary",)),
        name="convffn_sample",
    )(x, w_up16, conv_w, conv_b, w_dn16, ln_g, ln_b, p1, p2)


def kernel(x_prompt, x_sample, cache_k, cache_v, state_hgrn, state_ffn_conv, page_table,
           ln_in_g, ln_in_b, w_in, hgrn_lb_logits, hgrn_norm_w,
           lambda_q1, lambda_k1, lambda_q2, lambda_k2, diff_norm_w, w_o,
           ln1_g, ln1_b, w_up, conv_w, conv_b, w_down, ln2_g, ln2_b):
    bp, tp_, _ = x_prompt.shape
    bs, ts, _ = x_sample.shape
    pad_t = SAMPLE_PAD_T
    n_phys, page = cache_k.shape[1], cache_k.shape[2]
    slab = page * H_B
    np_rows, ns_rows = bp * tp_, bs * pad_t

    w_in16 = w_in.astype(BF16)
    w_o16 = w_o.astype(BF16)
    w_up16 = w_up.astype(BF16)
    w_dn16 = w_down.astype(BF16)
    lam_vecs = jnp.stack([lambda_q1, lambda_k1, lambda_q2, lambda_k2], axis=1)
    ck = cache_k.reshape(DEPTH, n_phys, slab, 2 * DH_B)
    cv = cache_v.reshape(DEPTH, n_phys, slab, DV_B)
    s_zero = jnp.zeros((bp, H_A, DK_A, DV_A), F32)
    per_layer = lambda a: a.reshape(DEPTH, 1, a.shape[-1])
    hgrn_norm_w, diff_norm_w, conv_b = per_layer(hgrn_norm_w), per_layer(diff_norm_w), per_layer(conv_b)
    ln1_g, ln1_b, ln2_g, ln2_b = per_layer(ln1_g), per_layer(ln1_b), per_layer(ln2_g), per_layer(ln2_b)

    xp = _ln_in(x_prompt.reshape(np_rows, D_MODEL), ln_in_g, ln_in_b)
    xs = _ln_in(jnp.pad(x_sample, ((0, 0), (0, pad_t - ts), (0, 0))).reshape(ns_rows, D_MODEL),
                ln_in_g, ln_in_b)

    k_s, v_s, s_p, s_s, c_p, c_s = [], [], [], [], [], []
    kv_flat = ()
    for l in range(DEPTH):
        lam_init = 0.8 - 0.6 * math.exp(-0.3 * l)

        qa, logf, kc, ia, sg, qb, kb16, vb16, *kv_flat = _inproj(
            xp, w_in16, hgrn_lb_logits, l, kv_flat)
        seq = lambda a: a.reshape(bp, tp_, a.shape[-1])
        oa, sp = _hgrn(seq(qa), seq(logf), seq(kc), seq(ia), seq(sg), s_zero, hgrn_norm_w, l, tp_)
        ob = _attn_prompt(seq(qb), seq(kb16), seq(vb16), lam_vecs, diff_norm_w, l, lam_init)
        x1 = _oproj(oa.reshape(np_rows, W_A), ob.reshape(np_rows, W_B), xp, w_o16, ln1_g, ln1_b, l)
        x2, cp = _ffn_prompt(seq(x1), w_up16, conv_w, conv_b, w_dn16, ln2_g, ln2_b, l)
        xp = x2.reshape(np_rows, D_MODEL)
        s_p.append(sp)
        c_p.append(cp)

        qa, logf, kc, ia, sg, qb, kb, vb = _inproj(xs, w_in16, hgrn_lb_logits, l)
        seq = lambda a: a.reshape(bs, pad_t, a.shape[-1])
        oa, ss = _hgrn(seq(qa), seq(logf), seq(kc), seq(ia), seq(sg), state_hgrn[l],
                       hgrn_norm_w, l, ts)
        grow = lambda a: jnp.pad(a.reshape(bs, pad_t * H_B, DV_B),
                                 ((0, 0), (0, slab - pad_t * H_B), (0, 0)))
        ob = _attn_sample(seq(qb), grow(kb), grow(vb), ck, cv, page_table, lam_vecs,
                          diff_norm_w, l, lam_init, ts)
        x1 = _oproj(oa.reshape(ns_rows, W_A), ob.reshape(ns_rows, W_B), xs, w_o16, ln1_g, ln1_b, l)
        conv0 = state_ffn_conv[l]
        p1 = jnp.pad(conv0[:, 1:2], ((0, 0), (0, pad_t - 1), (0, 0))).reshape(ns_rows, D_FF)
        p2 = jnp.pad(conv0, ((0, 0), (0, pad_t - 2), (0, 0))).reshape(ns_rows, D_FF)
        xs, a_s = _ffn_sample(x1, w_up16, conv_w, conv_b, w_dn16, ln2_g, ln2_b, p1, p2, l, pad_t)
        k_s.append(seq(kb)[:, :ts].reshape(bs, ts, H_B, 2 * DH_B))
        v_s.append(seq(vb)[:, :ts].reshape(bs, ts, H_B, DV_B))
        s_s.append(ss)
        c_s.append(a_s.reshape(bs, pad_t, D_FF)[:, ts - (CONV_W - 1):ts])

    y_p = xp.reshape(bp, tp_, D_MODEL)
    y_s = xs.reshape(bs, pad_t, D_MODEL)[:, :ts]
    k_p, v_p = (a.reshape(DEPTH, bp, tp_, H_B, DV_B) for a in kv_flat)
    return (y_p, y_s, k_p, v_p, jnp.stack(k_s), jnp.stack(v_s),
            jnp.stack(s_p), jnp.stack(s_s), jnp.stack(c_p), jnp.stack(c_s))
```

```python
import functools
import math

import jax
import jax.numpy as jnp
from jax import lax
from jax.experimental import pallas as pl
from jax.experimental.pallas import tpu as pltpu

F32 = jnp.float32
BF16 = jnp.bfloat16

D_MODEL = 1024
DEPTH = 4
H_A, DK_A, DV_A = 4, 128, 128
H_B, DH_B, DV_B = 4, 64, 128
W_A = H_A * DV_A
W_B = H_B * DV_B
IN_COLS = 2 * H_A * DK_A + 2 * W_A + 4 * H_B * DH_B + W_B
D_FF = 2816
FF_CHUNK = 256
FF_ROWS = 32
CONV_W = 3
CHUNK = 128
HGRN_SEQS_PER_STEP = 2
ALPHA = (2 * DEPTH) ** 0.25
LN_EPS = 1e-5
RMS_EPS = 1e-6
LOG2_E = 1.0 / math.log(2.0)
NEG = -1e30
LANES = 128
SAMPLE_PAD_T = 8
DEC_PAGES_PER_STEP = 16
VMEM_LIMIT = 56 * 1024 * 1024


def _cparams(sem):
    return pltpu.CompilerParams(dimension_semantics=sem, vmem_limit_bytes=VMEM_LIMIT)


def _layernorm(x, g, b):
    mu = jnp.mean(x, axis=-1, keepdims=True)
    xc = x - mu
    var = jnp.mean(xc * xc, axis=-1, keepdims=True)
    return xc * lax.rsqrt(var + LN_EPS) * g + b


def _silu(x):
    return x / (1.0 + jnp.exp2(x * (-LOG2_E)))


def _dot(a, b):
    return jnp.dot(a, b, preferred_element_type=F32)


def _dot_nt(a, b):
    return lax.dot_general(a, b, (((1,), (1,)), ((), ())), preferred_element_type=F32)


def _dot_tn(a, b):
    return lax.dot_general(a, b, (((0,), (0,)), ((), ())), preferred_element_type=F32)


def _lambda(lam_ref, lam_init):
    lv = lam_ref[...]
    p1 = jnp.sum(lv[0:1] * lv[1:2], axis=-1, keepdims=True)
    p2 = jnp.sum(lv[2:3] * lv[3:4], axis=-1, keepdims=True)
    return jnp.exp(p1) - jnp.exp(p2) + lam_init


def _diff_norm(acc1, l1, acc2, l2, lam, nw, lam_init):
    o = acc1 / l1 - lam * (acc2 / l2)
    ms = jnp.mean(o * o, axis=-1, keepdims=True)
    return o * lax.rsqrt(ms + RMS_EPS) * nw * (1.0 - lam_init)


def _ln_kernel(x_ref, g_ref, b_ref, o_ref):
    o_ref[...] = _layernorm(x_ref[...], g_ref[...], b_ref[...])


def _ln_in(x, g, b):
    n = x.shape[0]
    tm = min(512, n)
    return pl.pallas_call(
        _ln_kernel,
        grid=(n // tm,),
        in_specs=[pl.BlockSpec((tm, D_MODEL), lambda i: (i, 0)),
                  pl.BlockSpec((1, D_MODEL), lambda i: (0, 0)),
                  pl.BlockSpec((1, D_MODEL), lambda i: (0, 0))],
        out_specs=pl.BlockSpec((tm, D_MODEL), lambda i: (i, 0)),
        out_shape=jax.ShapeDtypeStruct((n, D_MODEL), F32),
        compiler_params=_cparams(("parallel",)),
        name="ln_in",
    )(x, g.reshape(1, D_MODEL), b.reshape(1, D_MODEL))


def _inproj_kernel(*refs, layer, flat_kv):
    if flat_kv:
        if layer == 0:
            x_ref, w_ref, lbl_ref = refs[:3]
        else:
            x_ref, w_ref, lbl_ref, _, _ = refs[:5]
        (qa_ref, logf_ref, kc_ref, ia_ref, sg_ref, qb_ref,
         kb16_ref, vb16_ref, kflat_ref, vflat_ref) = refs[-10:]
    else:
        (x_ref, w_ref, lbl_ref, qa_ref, logf_ref, kc_ref, ia_ref, sg_ref,
         qb_ref, kb_ref, vb_ref) = refs
    x = x_ref[...].astype(BF16)
    tm = x.shape[0]

    def proj(c0, n):
        return _dot(x, w_ref[:, c0:c0 + n])

    qa = proj(0, W_A)
    qa_ref[...] = (_silu(qa) * (DK_A ** -0.5)).astype(BF16)

    fa = proj(W_A, W_A)
    log_sig = jnp.minimum(fa, 0.0) - jnp.log1p(jnp.exp(-jnp.abs(fa)))
    sig_neg = 1.0 / (1.0 + jnp.exp(fa))
    if layer == 0:
        logf_ref[...] = log_sig
        kc_ref[...] = sig_neg.astype(BF16)
    else:
        lg = lbl_ref[...]
        e = jnp.exp(lg - jnp.max(lg, axis=0, keepdims=True))
        sm = e / jnp.sum(e, axis=0, keepdims=True)
        lb = sm[1:2]
        for j in range(2, layer + 1):
            lb = lb + sm[j:j + 1]
        u = jnp.log(lb)
        w = jnp.log1p(-lb) + log_sig
        logf_ref[...] = jnp.maximum(u, w) + jnp.log1p(jnp.exp(-jnp.abs(u - w)))
        kc_ref[...] = ((1.0 - lb) * sig_neg).astype(BF16)

    ia_ref[...] = proj(2 * W_A, W_A).astype(BF16)
    sg_ref[...] = _silu(proj(3 * W_A, W_A)).astype(BF16)
    qb_ref[...] = (proj(4 * W_A, W_B) * (DH_B ** -0.5)).astype(qb_ref.dtype)
    kb = proj(4 * W_A + W_B, W_B)
    vb = proj(4 * W_A + 2 * W_B, W_B)
    if flat_kv:
        kb16_ref[...] = kb.astype(BF16)
        vb16_ref[...] = vb.astype(BF16)
        for h in range(H_B):
            hs = slice(h * DV_B, (h + 1) * DV_B)
            kflat_ref[pl.ds(h, tm, stride=H_B), :] = kb[:, hs]
            vflat_ref[pl.ds(h, tm, stride=H_B), :] = vb[:, hs]
    else:
        kb_ref[...] = kb
        vb_ref[...] = vb


def _inproj(x, w_in16, lb_logits, layer, kv_flat=None):
    n = x.shape[0]
    tm = min(512, n)
    row = lambda i: (i, 0)
    f32_out = jax.ShapeDtypeStruct((n, W_A), F32)
    bf_out = jax.ShapeDtypeStruct((n, W_A), BF16)
    blk = pl.BlockSpec((tm, W_A), row)
    in_specs = [pl.BlockSpec((tm, D_MODEL), row),
                pl.BlockSpec((None, D_MODEL, IN_COLS), lambda i: (layer, 0, 0),
                             pipeline_mode=pl.Buffered(1)),
                pl.BlockSpec((DEPTH, W_A), lambda i: (0, 0))]
    if kv_flat is None:
        out_specs = [blk] * 8
        out_shape = [bf_out, f32_out, bf_out, bf_out, bf_out] + [f32_out] * 3
        args, aliases = (), {}
    else:
        flat = jax.ShapeDtypeStruct((DEPTH, n * H_B, DV_B), F32)
        out_specs = [blk] * 8 + [pl.BlockSpec((None, tm * H_B, DV_B), lambda i: (layer, i, 0))] * 2
        out_shape = [bf_out, f32_out] + [bf_out] * 6 + [flat, flat]
        args = tuple(kv_flat)
        in_specs += [pl.BlockSpec(memory_space=pl.ANY)] * len(args)
        aliases = {3: 8, 4: 9} if args else {}
    return pl.pallas_call(
        functools.partial(_inproj_kernel, layer=layer, flat_kv=kv_flat is not None),
        grid=(n // tm,),
        in_specs=in_specs,
        out_specs=out_specs,
        out_shape=out_shape,
        input_output_aliases=aliases,
        compiler_params=_cparams(("parallel",)),
        name="inproj",
    )(x, w_in16, lb_logits, *args)


def _boundary_rows(b, m, rows):
    c, w = b.shape
    if 2 * m >= 8:
        parts = [jnp.broadcast_to(b[p * 2 * m + m - 1:p * 2 * m + m], (2 * m, w))
                 for p in range(c // (2 * m))]
        return parts[0] if len(parts) == 1 else jnp.concatenate(parts, axis=0)
    if m == 2:
        r = rows % 4
        return jnp.where(r == 0, pltpu.roll(b, c - 1, 0),
                         jnp.where(r == 1, b,
                                   jnp.where(r == 2, pltpu.roll(b, 1, 0), pltpu.roll(b, 2, 0))))
    return jnp.where(rows % 2 == 0, b, pltpu.roll(b, 1, 0))


def _hgrn_kernel(q_ref, g_ref, kc_ref, v_ref, sg_ref, s0_ref, nw_ref, o_ref, s_ref, st_scr,
                 *, chunk, t_valid, t_total):
    ci = pl.program_id(1)
    n_seq = q_ref.shape[0]

    @pl.when(ci == 0)
    def _():
        for sq in range(n_seq):
            st_scr[sq] = jnp.concatenate([s0_ref[sq, h].T for h in range(H_A)], axis=1)

    for sq in range(n_seq):
        _hgrn_chunk(sq, ci, q_ref, g_ref, kc_ref, v_ref, sg_ref, nw_ref, o_ref, st_scr,
                    chunk=chunk, t_valid=t_valid, t_total=t_total)

    @pl.when(ci == pl.num_programs(1) - 1)
    def _():
        for sq in range(n_seq):
            for h in range(H_A):
                s_ref[sq, h] = st_scr[sq, :, h * DK_A:(h + 1) * DK_A].T


def _hgrn_chunk(sq, ci, q_ref, g_ref, kc_ref, v_ref, sg_ref, nw_ref, o_ref, st_scr,
                *, chunk, t_valid, t_total):
    rows = lax.broadcasted_iota(jnp.int32, (chunk, 1), 0)
    t_idx = lax.broadcasted_iota(jnp.int32, (chunk, chunk), 0)
    s_idx = lax.broadcasted_iota(jnp.int32, (chunk, chunk), 1)
    tri = (s_idx <= t_idx).astype(F32)

    q = q_ref[sq].astype(F32)
    g = g_ref[sq]
    kc = kc_ref[sq].astype(F32)
    if t_valid < t_total:
        valid = (ci * chunk + rows) < t_valid
        g = jnp.where(valid, g, 0.0)
        kc = jnp.where(valid, kc, 0.0)
    v16 = v_ref[sq]
    heads = [slice(h * DK_A, (h + 1) * DK_A) for h in range(H_A)]

    b = jnp.dot(tri, g, preferred_element_type=F32, precision=lax.Precision.HIGHEST) * LOG2_E

    level = jnp.where(s_idx <= t_idx, 31 - lax.clz(t_idx ^ s_idx), -2)
    q16 = q.astype(BF16)
    k16 = kc.astype(BF16)
    a = [jnp.where(level == -1, _dot_nt(q16[:, hs], k16[:, hs]), 0.0) for hs in heads]
    m = chunk // 2
    while m >= 1:
        r = _boundary_rows(b, m, rows)
        second_half = (rows // m) % 2 == 1
        x = (jnp.where(second_half, q, kc) * jnp.exp2(-jnp.abs(b - r))).astype(BF16)
        keep = level == (m.bit_length() - 1)
        a = [jnp.where(keep, _dot_nt(x[:, hs], x[:, hs]), a[h]) for h, hs in enumerate(heads)]
        m //= 2

    qe = (q * jnp.exp2(b)).astype(BF16)
    b_last = b[chunk - 1:chunk]
    decay = jnp.exp2(b_last)
    k_last = (kc * jnp.exp2(b_last - b)).astype(BF16)
    st = st_scr[sq]
    st16 = st.astype(BF16)
    o = [_dot_nt(qe[:, hs], st16[:, hs]) + _dot(a[h].astype(BF16), v16[:, hs])
         for h, hs in enumerate(heads)]
    st_new = st * decay + jnp.concatenate(
        [_dot_tn(v16[:, hs], k_last[:, hs]) for hs in heads], axis=1)
    st_scr[sq] = st_new
    nw = nw_ref[...]
    scale = jnp.concatenate(
        [jnp.broadcast_to(lax.rsqrt(jnp.mean(oh * oh, axis=-1, keepdims=True) + RMS_EPS),
                          (chunk, DV_A)) * nw for oh in o], axis=1)
    o_ref[sq] = (jnp.concatenate(o, axis=1) * scale * sg_ref[sq].astype(F32)).astype(o_ref.dtype)


def _hgrn(qa, logf, kc, ia, sg, s0, norm_w, layer, t_valid):
    bsz, t, _ = qa.shape
    chunk = CHUNK if t % CHUNK == 0 else t
    n_seq = math.gcd(bsz, HGRN_SEQS_PER_STEP if chunk == CHUNK else 2 * HGRN_SEQS_PER_STEP)
    seq = pl.BlockSpec((n_seq, chunk, W_A), lambda b, c: (b, c, 0))
    state = pl.BlockSpec((n_seq, H_A, DK_A, DV_A), lambda b, c: (b, 0, 0, 0))
    return pl.pallas_call(
        functools.partial(_hgrn_kernel, chunk=chunk, t_valid=t_valid, t_total=t),
        grid=(bsz // n_seq, t // chunk),
        in_specs=[seq, seq, seq, seq, seq, state,
                  pl.BlockSpec((None, 1, DV_A), lambda b, c: (layer, 0, 0))],
        out_specs=[seq, state],
        out_shape=[jax.ShapeDtypeStruct((bsz, t, W_A), BF16),
                   jax.ShapeDtypeStruct((bsz, H_A, DK_A, DV_A), F32)],
        scratch_shapes=[pltpu.VMEM((n_seq, DV_A, H_A * DK_A), F32)],
        compiler_params=_cparams(("parallel", "arbitrary")),
        name="hgrn2",
    )(qa, logf, kc, ia, sg, s0, norm_w)


def _head_slope(h):
    return jnp.where(h == 0, 2.0 ** -2, jnp.where(h == 1, 2.0 ** -4,
                     jnp.where(h == 2, 2.0 ** -6, 2.0 ** -8))).astype(F32)


def _fold_lanes(x, op):
    out = x[:, 0:LANES]
    for c in range(LANES, x.shape[1], LANES):
        out = op(out, x[:, c:c + LANES])
    return out


def _attn_kernel(q_ref, k_ref, v_ref, lam_ref, nw_ref, o_ref, *, tq, n_q, lam_init):
    h = pl.program_id(1)
    slope = _head_slope(h)
    lam = _lambda(lam_ref, lam_init)
    nw = nw_ref[...]
    lane = lax.broadcasted_iota(jnp.int32, (tq, 2 * DH_B), 1)
    k_diag = lax.broadcasted_iota(jnp.int32, (1, tq), 1)
    causal = k_diag <= lax.broadcasted_iota(jnp.int32, (2 * tq, 1), 0) % tq
    for i in range(n_q):
        qf = q_ref[i * tq:(i + 1) * tq, :].astype(F32)
        q2 = jnp.concatenate([jnp.where(lane < DH_B, qf, 0.0),
                              jnp.where(lane >= DH_B, qf, 0.0)], axis=0).astype(BF16)
        s_d = _dot_nt(q2, k_ref[i * tq:(i + 1) * tq, :]) + slope * k_diag.astype(F32)
        s_d = jnp.where(causal, s_d, NEG)
        m = _fold_lanes(s_d, jnp.maximum)
        if i > 0:
            k_off = lax.broadcasted_iota(jnp.int32, (1, i * tq), 1) - i * tq
            s_o = _dot_nt(q2, k_ref[0:i * tq, :]) + slope * k_off.astype(F32)
            m = jnp.maximum(m, _fold_lanes(s_o, jnp.maximum))
        m = jnp.max(m, axis=-1, keepdims=True)
        p_d = jnp.exp(s_d - m)
        l = _fold_lanes(p_d, jnp.add)
        acc = _dot(p_d.astype(BF16), v_ref[i * tq:(i + 1) * tq, :])
        if i > 0:
            p_o = jnp.exp(s_o - m)
            l = l + _fold_lanes(p_o, jnp.add)
            acc = acc + _dot(p_o.astype(BF16), v_ref[0:i * tq, :])
        l = jnp.sum(l, axis=-1, keepdims=True)
        o = _diff_norm(acc[0:tq], l[0:tq], acc[tq:2 * tq], l[tq:2 * tq], lam, nw, lam_init)
        o_ref[i * tq:(i + 1) * tq, :] = o.astype(o_ref.dtype)


def _attn_prompt(qb16, kb16, vb16, lam_vecs, norm_w, layer, lam_init):
    bsz, t, _ = qb16.shape
    tq = min(256, t)
    seq_head = pl.BlockSpec((None, t, DV_B), lambda b, h: (b, 0, h))
    return pl.pallas_call(
        functools.partial(_attn_kernel, tq=tq, n_q=t // tq, lam_init=lam_init),
        grid=(bsz, H_B),
        in_specs=[seq_head, seq_head, seq_head,
                  pl.BlockSpec((None, 4, DH_B), lambda b, h: (layer, 0, 0)),
                  pl.BlockSpec((None, 1, DV_B), lambda b, h: (layer, 0, 0))],
        out_specs=seq_head,
        out_shape=jax.ShapeDtypeStruct((bsz, t, W_B), BF16),
        compiler_params=_cparams(("parallel", "parallel")),
        name="diff_attn_prompt",
    )(qb16, kb16, vb16, lam_vecs, norm_w)


def _softmax_update(s, v16, m_scr, l_scr, acc_scr):
    m_prev = m_scr[...]
    m_new = jnp.maximum(m_prev, jnp.max(s, axis=-1, keepdims=True))
    alpha = jnp.exp(m_prev - m_new)
    p = jnp.exp(s - m_new)
    l_scr[...] = alpha * l_scr[...] + jnp.sum(p, axis=-1, keepdims=True)
    acc_scr[...] = alpha * acc_scr[...] + _dot(p.astype(BF16), v16)
    m_scr[...] = m_new


def _dec_kernel(pt_ref, q_ref, kn_ref, vn_ref, lam_ref, nw_ref, *rest,
                n_pg, page, past_len, lam_init, t_valid):
    del pt_ref
    k_refs = rest[:n_pg]
    v_refs = rest[n_pg:2 * n_pg]
    o_ref = rest[2 * n_pg]
    qx_scr, m_scr, l_scr, acc_scr = rest[2 * n_pg + 1:]
    j = pl.program_id(1)
    tp = SAMPLE_PAD_T
    slab = page * H_B
    n_rows = H_B * 2 * tp
    row = lax.broadcasted_iota(jnp.int32, (n_rows, 1), 0)
    row_head = row // (2 * tp)
    slope = jnp.where(row_head == 0, 2.0 ** -2, jnp.where(row_head == 1, 2.0 ** -4,
                      jnp.where(row_head == 2, 2.0 ** -6, 2.0 ** -8))).astype(F32)

    @pl.when(j == 0)
    def _():
        qf = q_ref[...]
        lane = lax.broadcasted_iota(jnp.int32, (tp, 2 * DH_B), 1)
        parts = []
        for h in range(H_B):
            qh = qf[:, h * 2 * DH_B:(h + 1) * 2 * DH_B]
            parts.append(jnp.where(lane < DH_B, qh, 0.0))
            parts.append(jnp.where(lane >= DH_B, qh, 0.0))
        qx_scr[...] = jnp.concatenate(parts, axis=0).astype(BF16)
        m_scr[...] = jnp.full(m_scr.shape, NEG, F32)
        l_scr[...] = jnp.zeros(l_scr.shape, F32)
        acc_scr[...] = jnp.zeros(acc_scr.shape, F32)

    qx = qx_scr[...]
    col = lax.broadcasted_iota(jnp.int32, (1, slab), 1)
    own_head = col % H_B == row_head
    pos0 = col // H_B + (j * (n_pg * page) - past_len)
    s = [jnp.where(own_head,
                   _dot_nt(qx, k_refs[p][...].astype(BF16)) + slope * (pos0 + p * page).astype(F32),
                   NEG) for p in range(n_pg)]
    s_max = s[0]
    for sp in s[1:]:
        s_max = jnp.maximum(s_max, sp)
    m_prev = m_scr[...]
    m_new = jnp.maximum(m_prev, jnp.max(s_max, axis=-1, keepdims=True))
    alpha = jnp.exp(m_prev - m_new)
    pr = [jnp.exp(sp - m_new) for sp in s]
    p_sum = pr[0]
    for pp in pr[1:]:
        p_sum = p_sum + pp
    pv = _dot(pr[0].astype(BF16), v_refs[0][...].astype(BF16))
    for p in range(1, n_pg):
        pv = pv + _dot(pr[p].astype(BF16), v_refs[p][...].astype(BF16))
    l_scr[...] = alpha * l_scr[...] + jnp.sum(p_sum, axis=-1, keepdims=True)
    acc_scr[...] = alpha * acc_scr[...] + pv
    m_scr[...] = m_new

    @pl.when(j == pl.num_programs(1) - 1)
    def _():
        col_n = lax.broadcasted_iota(jnp.int32, (1, slab), 1)
        t_k = col_n // H_B
        s_n = _dot_nt(qx, kn_ref[...].astype(BF16))
        ok = (col_n % H_B == row_head) & (t_k <= row % tp) & (t_k < t_valid)
        s_n = jnp.where(ok, s_n + slope * t_k.astype(F32), NEG)
        _softmax_update(s_n, vn_ref[...].astype(BF16), m_scr, l_scr, acc_scr)
        lam = _lambda(lam_ref, lam_init)
        nw = nw_ref[...]
        for h in range(H_B):
            r0 = h * 2 * tp
            o = _diff_norm(acc_scr[r0:r0 + tp], l_scr[r0:r0 + tp],
                           acc_scr[r0 + tp:r0 + 2 * tp], l_scr[r0 + tp:r0 + 2 * tp],
                           lam, nw, lam_init)
            o_ref[:, h * DV_B:(h + 1) * DV_B] = o.astype(o_ref.dtype)


def _attn_sample(qb, kn_pad, vn_pad, cache_k, cache_v, page_table, lam_vecs, norm_w,
                 layer, lam_init, t_valid):
    bsz, tp, _ = qb.shape
    n_pages = page_table.shape[1]
    slab = cache_k.shape[2]
    page = slab // H_B
    n_pg = math.gcd(n_pages, DEC_PAGES_PER_STEP)
    n_rows = H_B * 2 * tp

    def page_spec(p):
        return pl.BlockSpec((None, None, slab, DV_B),
                            lambda b, j, pt: (layer, pt[b, j * n_pg + p], 0, 0))

    grid_spec = pltpu.PrefetchScalarGridSpec(
        num_scalar_prefetch=1,
        grid=(bsz, n_pages // n_pg),
        in_specs=[pl.BlockSpec((None, tp, W_B), lambda b, j, pt: (b, 0, 0)),
                  pl.BlockSpec((None, slab, DV_B), lambda b, j, pt: (b, 0, 0)),
                  pl.BlockSpec((None, slab, DV_B), lambda b, j, pt: (b, 0, 0)),
                  pl.BlockSpec((None, 4, DH_B), lambda b, j, pt: (layer, 0, 0)),
                  pl.BlockSpec((None, 1, DV_B), lambda b, j, pt: (layer, 0, 0))]
                 + [page_spec(p) for p in range(n_pg)] * 2,
        out_specs=pl.BlockSpec((None, tp, W_B), lambda b, j, pt: (b, 0, 0)),
        scratch_shapes=[pltpu.VMEM((n_rows, 2 * DH_B), BF16),
                        pltpu.VMEM((n_rows, 1), F32), pltpu.VMEM((n_rows, 1), F32),
                        pltpu.VMEM((n_rows, DV_B), F32)],
    )
    return pl.pallas_call(
        functools.partial(_dec_kernel, n_pg=n_pg, page=page, past_len=n_pages * page,
                          lam_init=lam_init, t_valid=t_valid),
        grid_spec=grid_spec,
        out_shape=jax.ShapeDtypeStruct((bsz, tp, W_B), F32),
        compiler_params=_cparams(("parallel", "arbitrary")),
        name="diff_attn_sample",
    )(page_table, qb, kn_pad, vn_pad, lam_vecs, norm_w,
      *([cache_k] * n_pg), *([cache_v] * n_pg))


def _ffn_kernel(*refs, tm, seq_len, per_row_state):
    (x_ref, oa_ref, ob_ref, wo_ref, g1_ref, b1_ref,
     wup_ref, cw_ref, cb_ref, wdn_ref, g_ref, b_ref) = refs[:12]
    if per_row_state:
        p1_ref, p2_ref, o_ref, a_ref, h_scr = refs[12:]
    else:
        o_ref, cs_ref, h_scr, a_scr, g_scr = refs[12:]
        ti = pl.program_id(1)

        @pl.when(ti == 0)
        def _():
            a_scr[0:8, :] = jnp.zeros((8, D_FF), F32)

    mixed = (_dot(oa_ref[...].astype(BF16), wo_ref[0:W_A, :])
             + _dot(ob_ref[...].astype(BF16), wo_ref[W_A:W_A + W_B, :]))
    x = _layernorm(ALPHA * x_ref[...] + mixed, g1_ref[...], b1_ref[...])
    x16 = x.astype(BF16)
    for c0 in range(0, D_FF, FF_CHUNK):
        cs = slice(c0, c0 + FF_CHUNK)
        a = _dot(x16, wup_ref[:, cs])
        g = _dot(x16, wup_ref[:, D_FF + c0:D_FF + c0 + FF_CHUNK])
        cw = cw_ref[:, cs]
        cb = cb_ref[:, cs]
        if per_row_state:
            t = lax.broadcasted_iota(jnp.int32, (tm, 1), 0) % seq_len
            am1 = jnp.where(t >= 1, pltpu.roll(a, 1, 0), p1_ref[:, cs])
            am2 = jnp.where(t >= 2, pltpu.roll(a, 2, 0), p2_ref[:, cs])
            a_ref[:, cs] = a
            c = cb + am2 * cw[0:1] + am1 * cw[1:2] + a * cw[2:3]
            h_scr[:, cs] = (_silu(c) * g).astype(BF16)
        else:
            a_scr[8:8 + tm, cs] = a
            g_scr[:, cs] = g
            for r0 in range(0, tm, FF_ROWS):
                c = (cb + a_scr[6 + r0:6 + r0 + FF_ROWS, cs] * cw[0:1]
                     + a_scr[7 + r0:7 + r0 + FF_ROWS, cs] * cw[1:2]
                     + a_scr[8 + r0:8 + r0 + FF_ROWS, cs] * cw[2:3])
                h_scr[r0:r0 + FF_ROWS, cs] = (
                    _silu(c) * g_scr[r0:r0 + FF_ROWS, cs]).astype(BF16)
    if not per_row_state:
        @pl.when(ti == pl.num_programs(1) - 1)
        def _():
            cs_ref[...] = a_scr[6 + tm:8 + tm, :]

        a_scr[0:8, :] = a_scr[tm:tm + 8, :]
    y = _dot(h_scr[...], wdn_ref[...])
    o_ref[...] = _layernorm(ALPHA * x + y, g_ref[...], b_ref[...])


def _ffn_specs(layer):
    once = pl.Buffered(1)
    return [pl.BlockSpec((None, D_MODEL, D_MODEL), lambda *a: (layer, 0, 0), pipeline_mode=once),
            pl.BlockSpec((None, 1, D_MODEL), lambda *a: (layer, 0, 0)),
            pl.BlockSpec((None, 1, D_MODEL), lambda *a: (layer, 0, 0)),
            pl.BlockSpec((None, D_MODEL, 2 * D_FF), lambda *a: (layer, 0, 0), pipeline_mode=once),
            pl.BlockSpec((None, CONV_W, D_FF), lambda *a: (layer, 0, 0)),
            pl.BlockSpec((None, 1, D_FF), lambda *a: (layer, 0, 0)),
            pl.BlockSpec((None, D_FF, D_MODEL), lambda *a: (layer, 0, 0), pipeline_mode=once),
            pl.BlockSpec((None, 1, D_MODEL), lambda *a: (layer, 0, 0)),
            pl.BlockSpec((None, 1, D_MODEL), lambda *a: (layer, 0, 0))]


def _ffn_prompt(x, oa, ob, weights, layer):
    bsz, t, _ = x.shape
    tm = min(512, t)
    tile = lambda w: pl.BlockSpec((None, tm, w), lambda b, i: (b, i, 0))
    return pl.pallas_call(
        functools.partial(_ffn_kernel, tm=tm, seq_len=t, per_row_state=False),
        grid=(bsz, t // tm),
        in_specs=[tile(D_MODEL), tile(W_A), tile(W_B)] + _ffn_specs(layer),
        out_specs=[pl.BlockSpec((None, tm, D_MODEL), lambda b, i: (b, i, 0)),
                   pl.BlockSpec((None, CONV_W - 1, D_FF), lambda b, i: (b, 0, 0))],
        out_shape=[jax.ShapeDtypeStruct((bsz, t, D_MODEL), F32),
                   jax.ShapeDtypeStruct((bsz, CONV_W - 1, D_FF), F32)],
        scratch_shapes=[pltpu.VMEM((tm, D_FF), BF16), pltpu.VMEM((8 + tm, D_FF), F32),
                        pltpu.VMEM((tm, D_FF), F32)],
        compiler_params=_cparams(("parallel", "arbitrary")),
        name="convffn_prompt",
    )(x, oa, ob, *weights)


def _ffn_sample(x, oa, ob, weights, p1, p2, layer, seq_len):
    n = x.shape[0]
    full = lambda i: (0, 0)
    whole = lambda w: pl.BlockSpec((n, w), full)
    return pl.pallas_call(
        functools.partial(_ffn_kernel, tm=n, seq_len=seq_len, per_row_state=True),
        grid=(1,),
        in_specs=[whole(D_MODEL), whole(W_A), whole(W_B)] + _ffn_specs(layer)
                 + [whole(D_FF), whole(D_FF)],
        out_specs=[pl.BlockSpec((n, D_MODEL), full), pl.BlockSpec((n, D_FF), full)],
        out_shape=[jax.ShapeDtypeStruct((n, D_MODEL), F32),
                   jax.ShapeDtypeStruct((n, D_FF), F32)],
        scratch_shapes=[pltpu.VMEM((n, D_FF), BF16)],
        compiler_params=_cparams(("arbitrary",)),
        name="convffn_sample",
    )(x, oa, ob, *weights, p1, p2)


def kernel(x_prompt, x_sample, cache_k, cache_v, state_hgrn, state_ffn_conv, page_table,
           ln_in_g, ln_in_b, w_in, hgrn_lb_logits, hgrn_norm_w,
           lambda_q1, lambda_k1, lambda_q2, lambda_k2, diff_norm_w, w_o,
           ln1_g, ln1_b, w_up, conv_w, conv_b, w_down, ln2_g, ln2_b):
    bp, tp_, _ = x_prompt.shape
    bs, ts, _ = x_sample.shape
    pad_t = SAMPLE_PAD_T
    n_phys, page = cache_k.shape[1], cache_k.shape[2]
    slab = page * H_B
    np_rows, ns_rows = bp * tp_, bs * pad_t

    w_in16 = w_in.astype(BF16)
    w_o16 = w_o.astype(BF16)
    w_up16 = w_up.astype(BF16)
    w_dn16 = w_down.astype(BF16)
    lam_vecs = jnp.stack([lambda_q1, lambda_k1, lambda_q2, lambda_k2], axis=1)
    ck = cache_k.reshape(DEPTH, n_phys, slab, 2 * DH_B)
    cv = cache_v.reshape(DEPTH, n_phys, slab, DV_B)
    s_zero = jnp.zeros((bp, H_A, DK_A, DV_A), F32)
    per_layer = lambda a: a.reshape(DEPTH, 1, a.shape[-1])
    hgrn_norm_w, diff_norm_w = per_layer(hgrn_norm_w), per_layer(diff_norm_w)
    ffn_weights = (w_o16, per_layer(ln1_g), per_layer(ln1_b), w_up16, conv_w, per_layer(conv_b),
                   w_dn16, per_layer(ln2_g), per_layer(ln2_b))

    xp = _ln_in(x_prompt.reshape(np_rows, D_MODEL), ln_in_g, ln_in_b)
    xs = _ln_in(jnp.pad(x_sample, ((0, 0), (0, pad_t - ts), (0, 0))).reshape(ns_rows, D_MODEL),
                ln_in_g, ln_in_b)

    k_s, v_s, s_p, s_s, c_p, c_s = [], [], [], [], [], []
    kv_flat = ()
    for l in range(DEPTH):
        lam_init = 0.8 - 0.6 * math.exp(-0.3 * l)

        qa, logf, kc, ia, sg, qb, kb16, vb16, *kv_flat = _inproj(
            xp, w_in16, hgrn_lb_logits, l, kv_flat)
        seq = lambda a: a.reshape(bp, tp_, a.shape[-1])
        oa, sp = _hgrn(seq(qa), seq(logf), seq(kc), seq(ia), seq(sg), s_zero, hgrn_norm_w, l, tp_)
        ob = _attn_prompt(seq(qb), seq(kb16), seq(vb16), lam_vecs, diff_norm_w, l, lam_init)
        x2, cp = _ffn_prompt(seq(xp), oa, ob, ffn_weights, l)
        xp = x2.reshape(np_rows, D_MODEL)
        s_p.append(sp)
        c_p.append(cp)

        qa, logf, kc, ia, sg, qb, kb, vb = _inproj(xs, w_in16, hgrn_lb_logits, l)
        seq = lambda a: a.reshape(bs, pad_t, a.shape[-1])
        oa, ss = _hgrn(seq(qa), seq(logf), seq(kc), seq(ia), seq(sg), state_hgrn[l],
                       hgrn_norm_w, l, ts)
        grow = lambda a: jnp.pad(a.reshape(bs, pad_t * H_B, DV_B),
                                 ((0, 0), (0, slab - pad_t * H_B), (0, 0)))
        ob = _attn_sample(seq(qb), grow(kb), grow(vb), ck, cv, page_table, lam_vecs,
                          diff_norm_w, l, lam_init, ts)
        conv0 = state_ffn_conv[l]
        p1 = jnp.pad(conv0[:, 1:2], ((0, 0), (0, pad_t - 1), (0, 0))).reshape(ns_rows, D_FF)
        p2 = jnp.pad(conv0, ((0, 0), (0, pad_t - 2), (0, 0))).reshape(ns_rows, D_FF)
        xs, a_s = _ffn_sample(xs, oa.reshape(ns_rows, W_A), ob.reshape(ns_rows, W_B),
                              ffn_weights, p1, p2, l, pad_t)
        k_s.append(seq(kb)[:, :ts].reshape(bs, ts, H_B, 2 * DH_B))
        v_s.append(seq(vb)[:, :ts].reshape(bs, ts, H_B, DV_B))
        s_s.append(ss)
        c_s.append(a_s.reshape(bs, pad_t, D_FF)[:, ts - (CONV_W - 1):ts])

    y_p = xp.reshape(bp, tp_, D_MODEL)
    y_s = xs.reshape(bs, pad_t, D_MODEL)[:, :ts]
    k_p, v_p = (a.reshape(DEPTH, bp, tp_, H_B, DV_B) for a in kv_flat)
    return (y_p, y_s, k_p, v_p, jnp.stack(k_s), jnp.stack(v_s),
            jnp.stack(s_p), jnp.stack(s_s), jnp.stack(c_p), jnp.stack(c_s))
```

```python
import functools
import math

import jax
import jax.numpy as jnp
from jax import lax
from jax.experimental import pallas as pl
from jax.experimental.pallas import tpu as pltpu

F32 = jnp.float32
BF16 = jnp.bfloat16

D_MODEL = 1024
DEPTH = 4
H_A, DK_A, DV_A = 4, 128, 128
H_B, DH_B, DV_B = 4, 64, 128
W_A = H_A * DV_A
W_B = H_B * DV_B
IN_COLS = 2 * H_A * DK_A + 2 * W_A + 4 * H_B * DH_B + W_B
D_FF = 2816
FF_CHUNK = 256
FF_ROWS = 32
CONV_W = 3
CHUNK = 128
HGRN_SEQS_PER_STEP = 2
ALPHA = (2 * DEPTH) ** 0.25
LN_EPS = 1e-5
RMS_EPS = 1e-6
LOG2_E = 1.0 / math.log(2.0)
NEG = -1e30
LANES = 128
SAMPLE_PAD_T = 8
DEC_PAGES_PER_CHUNK = 8
DEC_SLOTS = 4
DEC_NEW_ROWS = 128
VMEM_LIMIT = 56 * 1024 * 1024


def _cparams(sem):
    return pltpu.CompilerParams(dimension_semantics=sem, vmem_limit_bytes=VMEM_LIMIT)


def _layernorm(x, g, b):
    mu = jnp.mean(x, axis=-1, keepdims=True)
    xc = x - mu
    var = jnp.mean(xc * xc, axis=-1, keepdims=True)
    return xc * lax.rsqrt(var + LN_EPS) * g + b


def _silu(x):
    return x / (1.0 + jnp.exp2(x * (-LOG2_E)))


def _dot(a, b):
    return jnp.dot(a, b, preferred_element_type=F32)


def _dot_nt(a, b):
    return lax.dot_general(a, b, (((1,), (1,)), ((), ())), preferred_element_type=F32)


def _dot_tn(a, b):
    return lax.dot_general(a, b, (((0,), (0,)), ((), ())), preferred_element_type=F32)


def _lambda(lam_ref, lam_init):
    lv = lam_ref[...]
    p1 = jnp.sum(lv[0:1] * lv[1:2], axis=-1, keepdims=True)
    p2 = jnp.sum(lv[2:3] * lv[3:4], axis=-1, keepdims=True)
    return jnp.exp(p1) - jnp.exp(p2) + lam_init


def _diff_norm(acc1, l1, acc2, l2, lam, nw, lam_init):
    o = acc1 / l1 - lam * (acc2 / l2)
    ms = jnp.mean(o * o, axis=-1, keepdims=True)
    return o * lax.rsqrt(ms + RMS_EPS) * nw * (1.0 - lam_init)


def _ln_kernel(x_ref, g_ref, b_ref, o_ref):
    o_ref[...] = _layernorm(x_ref[...], g_ref[...], b_ref[...])


def _ln_in(x, g, b):
    n = x.shape[0]
    tm = min(512, n)
    return pl.pallas_call(
        _ln_kernel,
        grid=(n // tm,),
        in_specs=[pl.BlockSpec((tm, D_MODEL), lambda i: (i, 0)),
                  pl.BlockSpec((1, D_MODEL), lambda i: (0, 0)),
                  pl.BlockSpec((1, D_MODEL), lambda i: (0, 0))],
        out_specs=pl.BlockSpec((tm, D_MODEL), lambda i: (i, 0)),
        out_shape=jax.ShapeDtypeStruct((n, D_MODEL), F32),
        compiler_params=_cparams(("parallel",)),
        name="ln_in",
    )(x, g.reshape(1, D_MODEL), b.reshape(1, D_MODEL))


def _inproj_kernel(*refs, layer, flat_kv):
    if flat_kv:
        x_ref, w_ref, lbl_ref, _, _ = refs[:5]
        (qa_ref, logf_ref, kc_ref, ia_ref, sg_ref, qb_ref,
         kb16_ref, vb16_ref, kflat_ref, vflat_ref) = refs[-10:]
    else:
        (x_ref, w_ref, lbl_ref, qa_ref, logf_ref, kc_ref, ia_ref, sg_ref,
         qb_ref, kb_ref, vb_ref) = refs
    x = x_ref[...].astype(BF16)
    tm = x.shape[0]

    def proj(c0, n):
        return _dot(x, w_ref[:, c0:c0 + n])

    qa = proj(0, W_A)
    qa_ref[...] = (_silu(qa) * (DK_A ** -0.5)).astype(BF16)

    fa = proj(W_A, W_A)
    log_sig = jnp.minimum(fa, 0.0) - jnp.log1p(jnp.exp(-jnp.abs(fa)))
    sig_neg = 1.0 / (1.0 + jnp.exp(fa))
    if layer == 0:
        logf_ref[...] = log_sig
        kc_ref[...] = sig_neg.astype(BF16)
    else:
        lg = lbl_ref[...]
        e = jnp.exp(lg - jnp.max(lg, axis=0, keepdims=True))
        sm = e / jnp.sum(e, axis=0, keepdims=True)
        lb = sm[1:2]
        for j in range(2, layer + 1):
            lb = lb + sm[j:j + 1]
        u = jnp.log(lb)
        w = jnp.log1p(-lb) + log_sig
        logf_ref[...] = jnp.maximum(u, w) + jnp.log1p(jnp.exp(-jnp.abs(u - w)))
        kc_ref[...] = ((1.0 - lb) * sig_neg).astype(BF16)

    ia_ref[...] = proj(2 * W_A, W_A).astype(BF16)
    sg_ref[...] = _silu(proj(3 * W_A, W_A)).astype(BF16)
    qb_ref[...] = (proj(4 * W_A, W_B) * (DH_B ** -0.5)).astype(qb_ref.dtype)
    kb = proj(4 * W_A + W_B, W_B)
    vb = proj(4 * W_A + 2 * W_B, W_B)
    if flat_kv:
        kb16_ref[...] = kb.astype(BF16)
        vb16_ref[...] = vb.astype(BF16)
        for h in range(H_B):
            hs = slice(h * DV_B, (h + 1) * DV_B)
            kflat_ref[pl.ds(h, tm, stride=H_B), :] = kb[:, hs]
            vflat_ref[pl.ds(h, tm, stride=H_B), :] = vb[:, hs]
    else:
        kb_ref[...] = kb
        vb_ref[...] = vb


def _inproj(x, w_in16, lb_logits, layer, kv_flat=None):
    n = x.shape[0]
    tm = min(512, n)
    row = lambda i: (i, 0)
    f32_out = jax.ShapeDtypeStruct((n, W_A), F32)
    bf_out = jax.ShapeDtypeStruct((n, W_A), BF16)
    blk = pl.BlockSpec((tm, W_A), row)
    in_specs = [pl.BlockSpec((tm, D_MODEL), row),
                pl.BlockSpec((None, D_MODEL, IN_COLS), lambda i: (layer, 0, 0),
                             pipeline_mode=pl.Buffered(1)),
                pl.BlockSpec((DEPTH, W_A), lambda i: (0, 0))]
    if kv_flat is None:
        out_specs = [blk] * 8
        out_shape = [bf_out, f32_out, bf_out, bf_out, bf_out] + [f32_out] * 3
        args, aliases = (), {}
    else:
        flat = jax.ShapeDtypeStruct((DEPTH, n * H_B, DV_B), F32)
        out_specs = [blk] * 8 + [pl.BlockSpec((None, tm * H_B, DV_B), lambda i: (layer, i, 0))] * 2
        out_shape = [bf_out, f32_out] + [bf_out] * 6 + [flat, flat]
        args = tuple(kv_flat)
        in_specs += [pl.BlockSpec(memory_space=pl.ANY)] * 2
        aliases = {3: 8, 4: 9}
    return pl.pallas_call(
        functools.partial(_inproj_kernel, layer=layer, flat_kv=kv_flat is not None),
        grid=(n // tm,),
        in_specs=in_specs,
        out_specs=out_specs,
        out_shape=out_shape,
        input_output_aliases=aliases,
        compiler_params=_cparams(("parallel",)),
        name="inproj",
    )(x, w_in16, lb_logits, *args)


def _boundary_rows(b, m, rows):
    c, w = b.shape
    if 2 * m >= 8:
        parts = [jnp.broadcast_to(b[p * 2 * m + m - 1:p * 2 * m + m], (2 * m, w))
                 for p in range(c // (2 * m))]
        return parts[0] if len(parts) == 1 else jnp.concatenate(parts, axis=0)
    if m == 2:
        r = rows % 4
        return jnp.where(r == 0, pltpu.roll(b, c - 1, 0),
                         jnp.where(r == 1, b,
                                   jnp.where(r == 2, pltpu.roll(b, 1, 0), pltpu.roll(b, 2, 0))))
    return jnp.where(rows % 2 == 0, b, pltpu.roll(b, 1, 0))


def _hgrn_kernel(q_ref, g_ref, kc_ref, v_ref, sg_ref, s0_ref, nw_ref, o_ref, s_ref, st_scr,
                 *, chunk, t_valid, t_total):
    ci = pl.program_id(1)
    n_seq = q_ref.shape[0]

    @pl.when(ci == 0)
    def _():
        for sq in range(n_seq):
            st_scr[sq] = jnp.concatenate([s0_ref[sq, h].T for h in range(H_A)], axis=1)

    for sq in range(n_seq):
        _hgrn_chunk(sq, ci, q_ref, g_ref, kc_ref, v_ref, sg_ref, nw_ref, o_ref, st_scr,
                    chunk=chunk, t_valid=t_valid, t_total=t_total)

    @pl.when(ci == pl.num_programs(1) - 1)
    def _():
        for sq in range(n_seq):
            for h in range(H_A):
                s_ref[sq, h] = st_scr[sq, :, h * DK_A:(h + 1) * DK_A].T


def _hgrn_chunk(sq, ci, q_ref, g_ref, kc_ref, v_ref, sg_ref, nw_ref, o_ref, st_scr,
                *, chunk, t_valid, t_total):
    rows = lax.broadcasted_iota(jnp.int32, (chunk, 1), 0)
    t_idx = lax.broadcasted_iota(jnp.int32, (chunk, chunk), 0)
    s_idx = lax.broadcasted_iota(jnp.int32, (chunk, chunk), 1)
    tri = (s_idx <= t_idx).astype(F32)

    q = q_ref[sq].astype(F32)
    g = g_ref[sq]
    kc = kc_ref[sq].astype(F32)
    if t_valid < t_total:
        valid = (ci * chunk + rows) < t_valid
        g = jnp.where(valid, g, 0.0)
        kc = jnp.where(valid, kc, 0.0)
    v16 = v_ref[sq]
    heads = [slice(h * DK_A, (h + 1) * DK_A) for h in range(H_A)]

    b = jnp.dot(tri, g, preferred_element_type=F32, precision=lax.Precision.HIGHEST) * LOG2_E

    level = jnp.where(s_idx <= t_idx, 31 - lax.clz(t_idx ^ s_idx), -2)
    q16 = q.astype(BF16)
    k16 = kc.astype(BF16)
    a = [jnp.where(level == -1, _dot_nt(q16[:, hs], k16[:, hs]), 0.0) for hs in heads]
    m = chunk // 2
    while m >= 1:
        r = _boundary_rows(b, m, rows)
        second_half = (rows // m) % 2 == 1
        x = (jnp.where(second_half, q, kc) * jnp.exp2(-jnp.abs(b - r))).astype(BF16)
        keep = level == (m.bit_length() - 1)
        a = [jnp.where(keep, _dot_nt(x[:, hs], x[:, hs]), a[h]) for h, hs in enumerate(heads)]
        m //= 2

    qe = (q * jnp.exp2(b)).astype(BF16)
    b_last = b[chunk - 1:chunk]
    decay = jnp.exp2(b_last)
    k_last = (kc * jnp.exp2(b_last - b)).astype(BF16)
    st = st_scr[sq]
    st16 = st.astype(BF16)
    o = [_dot_nt(qe[:, hs], st16[:, hs]) + _dot(a[h].astype(BF16), v16[:, hs])
         for h, hs in enumerate(heads)]
    st_new = st * decay + jnp.concatenate(
        [_dot_tn(v16[:, hs], k_last[:, hs]) for hs in heads], axis=1)
    st_scr[sq] = st_new
    nw = nw_ref[...]
    scale = jnp.concatenate(
        [jnp.broadcast_to(lax.rsqrt(jnp.mean(oh * oh, axis=-1, keepdims=True) + RMS_EPS),
                          (chunk, DV_A)) * nw for oh in o], axis=1)
    o_ref[sq] = (jnp.concatenate(o, axis=1) * scale * sg_ref[sq].astype(F32)).astype(o_ref.dtype)


def _hgrn(qa, logf, kc, ia, sg, s0, norm_w, layer, t_valid):
    bsz, t, _ = qa.shape
    chunk = CHUNK if t % CHUNK == 0 else t
    n_seq = math.gcd(bsz, HGRN_SEQS_PER_STEP if chunk == CHUNK else 2 * HGRN_SEQS_PER_STEP)
    seq = pl.BlockSpec((n_seq, chunk, W_A), lambda b, c: (b, c, 0))
    state = pl.BlockSpec((n_seq, H_A, DK_A, DV_A), lambda b, c: (b, 0, 0, 0))
    return pl.pallas_call(
        functools.partial(_hgrn_kernel, chunk=chunk, t_valid=t_valid, t_total=t),
        grid=(bsz // n_seq, t // chunk),
        in_specs=[seq, seq, seq, seq, seq, state,
                  pl.BlockSpec((None, 1, DV_A), lambda b, c: (layer, 0, 0))],
        out_specs=[seq, state],
        out_shape=[jax.ShapeDtypeStruct((bsz, t, W_A), BF16),
                   jax.ShapeDtypeStruct((bsz, H_A, DK_A, DV_A), F32)],
        scratch_shapes=[pltpu.VMEM((n_seq, DV_A, H_A * DK_A), F32)],
        compiler_params=_cparams(("parallel", "arbitrary")),
        name="hgrn2",
    )(qa, logf, kc, ia, sg, s0, norm_w)


def _head_slope(h):
    return jnp.where(h == 0, 2.0 ** -2, jnp.where(h == 1, 2.0 ** -4,
                     jnp.where(h == 2, 2.0 ** -6, 2.0 ** -8))).astype(F32)


def _fold_lanes(x, op):
    out = x[:, 0:LANES]
    for c in range(LANES, x.shape[1], LANES):
        out = op(out, x[:, c:c + LANES])
    return out


def _attn_kernel(q_ref, k_ref, v_ref, lam_ref, nw_ref, o_ref, *, tq, n_q, lam_init):
    h = pl.program_id(1)
    slope = _head_slope(h)
    lam = _lambda(lam_ref, lam_init)
    nw = nw_ref[...]
    lane = lax.broadcasted_iota(jnp.int32, (tq, 2 * DH_B), 1)
    k_diag = lax.broadcasted_iota(jnp.int32, (1, tq), 1)
    causal = k_diag <= lax.broadcasted_iota(jnp.int32, (2 * tq, 1), 0) % tq
    for i in range(n_q):
        qf = q_ref[i * tq:(i + 1) * tq, :].astype(F32)
        q2 = jnp.concatenate([jnp.where(lane < DH_B, qf, 0.0),
                              jnp.where(lane >= DH_B, qf, 0.0)], axis=0).astype(BF16)
        s_d = _dot_nt(q2, k_ref[i * tq:(i + 1) * tq, :]) + slope * k_diag.astype(F32)
        s_d = jnp.where(causal, s_d, NEG)
        m = _fold_lanes(s_d, jnp.maximum)
        if i > 0:
            k_off = lax.broadcasted_iota(jnp.int32, (1, i * tq), 1) - i * tq
            s_o = _dot_nt(q2, k_ref[0:i * tq, :]) + slope * k_off.astype(F32)
            m = jnp.maximum(m, _fold_lanes(s_o, jnp.maximum))
        m = jnp.max(m, axis=-1, keepdims=True)
        p_d = jnp.exp(s_d - m)
        l = _fold_lanes(p_d, jnp.add)
        acc = _dot(p_d.astype(BF16), v_ref[i * tq:(i + 1) * tq, :])
        if i > 0:
            p_o = jnp.exp(s_o - m)
            l = l + _fold_lanes(p_o, jnp.add)
            acc = acc + _dot(p_o.astype(BF16), v_ref[0:i * tq, :])
        l = jnp.sum(l, axis=-1, keepdims=True)
        o = _diff_norm(acc[0:tq], l[0:tq], acc[tq:2 * tq], l[tq:2 * tq], lam, nw, lam_init)
        o_ref[i * tq:(i + 1) * tq, :] = o.astype(o_ref.dtype)


def _attn_prompt(qb16, kb16, vb16, lam_vecs, norm_w, layer, lam_init):
    bsz, t, _ = qb16.shape
    tq = min(256, t)
    seq_head = pl.BlockSpec((None, t, DV_B), lambda b, h: (b, 0, h))
    return pl.pallas_call(
        functools.partial(_attn_kernel, tq=tq, n_q=t // tq, lam_init=lam_init),
        grid=(bsz, H_B),
        in_specs=[seq_head, seq_head, seq_head,
                  pl.BlockSpec((None, 4, DH_B), lambda b, h: (layer, 0, 0)),
                  pl.BlockSpec((None, 1, DV_B), lambda b, h: (layer, 0, 0))],
        out_specs=seq_head,
        out_shape=jax.ShapeDtypeStruct((bsz, t, W_B), BF16),
        compiler_params=_cparams(("parallel", "parallel")),
        name="diff_attn_prompt",
    )(qb16, kb16, vb16, lam_vecs, norm_w)


def _softmax_update(s, v16, m_scr, l_scr, acc_scr):
    m_prev = m_scr[...]
    m_new = jnp.maximum(m_prev, jnp.max(s, axis=-1, keepdims=True))
    alpha = jnp.exp(m_prev - m_new)
    p = jnp.exp(s - m_new)
    l_scr[...] = alpha * l_scr[...] + jnp.sum(p, axis=-1, keepdims=True)
    acc_scr[...] = alpha * acc_scr[...] + _dot(p.astype(BF16), v16)
    m_scr[...] = m_new


def _dec_kernel(pt_ref, q_ref, kn_ref, vn_ref, lam_ref, nw_ref, ck_hbm, cv_hbm, o_ref,
                kbuf, vbuf, sem, qx_scr, m_scr, l_scr, acc_scr,
                *, layer, n_pg, n_slots, chunks_per_seq, page, past_len, lam_init, t_valid):
    tp = SAMPLE_PAD_T
    slab = page * H_B
    n_rows = H_B * 2 * tp
    n_chunks = q_ref.shape[0] * chunks_per_seq
    row = lax.broadcasted_iota(jnp.int32, (n_rows, 1), 0)
    row_head = row // (2 * tp)
    slope = jnp.where(row_head == 0, 2.0 ** -2, jnp.where(row_head == 1, 2.0 ** -4,
                      jnp.where(row_head == 2, 2.0 ** -6, 2.0 ** -8))).astype(F32)
    col = lax.broadcasted_iota(jnp.int32, (1, slab), 1)
    own_head = col % H_B == row_head
    lam = _lambda(lam_ref, lam_init)
    nw = nw_ref[...]

    def page_copies(c, slot):
        b = c // chunks_per_seq
        first = (c % chunks_per_seq) * n_pg
        out = []
        for p in range(n_pg):
            pid = pt_ref[b, first + p]
            out.append(pltpu.make_async_copy(ck_hbm.at[layer, pid], kbuf.at[slot, p], sem.at[0, slot, p]))
            out.append(pltpu.make_async_copy(cv_hbm.at[layer, pid], vbuf.at[slot, p], sem.at[1, slot, p]))
        return out

    def start(c, slot):
        for cp in page_copies(c, slot):
            cp.start()

    def begin_sequence(b):
        qf = q_ref[b]
        lane = lax.broadcasted_iota(jnp.int32, (tp, 2 * DH_B), 1)
        parts = []
        for h in range(H_B):
            qh = qf[:, h * 2 * DH_B:(h + 1) * 2 * DH_B]
            parts.append(jnp.where(lane < DH_B, qh, 0.0))
            parts.append(jnp.where(lane >= DH_B, qh, 0.0))
        qx_scr[...] = jnp.concatenate(parts, axis=0).astype(BF16)
        m_scr[...] = jnp.full(m_scr.shape, NEG, F32)
        l_scr[...] = jnp.zeros(l_scr.shape, F32)
        acc_scr[...] = jnp.zeros(acc_scr.shape, F32)

    def end_sequence(b):
        col_n = lax.broadcasted_iota(jnp.int32, (1, kn_ref.shape[1]), 1)
        t_k = col_n // H_B
        s_n = _dot_nt(qx_scr[...], kn_ref[b].astype(BF16))
        ok = (col_n % H_B == row_head) & (t_k <= row % tp) & (t_k < t_valid)
        s_n = jnp.where(ok, s_n + slope * t_k.astype(F32), NEG)
        _softmax_update(s_n, vn_ref[b].astype(BF16), m_scr, l_scr, acc_scr)
        outs = []
        for h in range(H_B):
            r0 = h * 2 * tp
            outs.append(_diff_norm(acc_scr[r0:r0 + tp], l_scr[r0:r0 + tp],
                                   acc_scr[r0 + tp:r0 + 2 * tp], l_scr[r0 + tp:r0 + 2 * tp],
                                   lam, nw, lam_init))
        o_ref[b] = jnp.concatenate(outs, axis=1).astype(o_ref.dtype)

    for c in range(n_slots - 1):
        start(c, c)

    def group(it, carry):
        for u in range(n_slots):
            c = it * n_slots + u
            b = c // chunks_per_seq
            jc = c % chunks_per_seq

            @pl.when(c + n_slots - 1 < n_chunks)
            def _():
                start(c + n_slots - 1, (u + n_slots - 1) % n_slots)

            if u == 0:
                pl.when(jc == 0)(functools.partial(begin_sequence, b))

            for cp in page_copies(c, u):
                cp.wait()

            qx = qx_scr[...]
            pos0 = col // H_B + (jc * (n_pg * page) - past_len)
            s = [jnp.where(own_head,
                           _dot_nt(qx, kbuf[u, p].astype(BF16))
                           + slope * (pos0 + p * page).astype(F32), NEG) for p in range(n_pg)]
            s_max = s[0]
            for sp in s[1:]:
                s_max = jnp.maximum(s_max, sp)
            m_prev = m_scr[...]
            m_new = jnp.maximum(m_prev, jnp.max(s_max, axis=-1, keepdims=True))
            alpha = jnp.exp(m_prev - m_new)
            pr = [jnp.exp(sp - m_new) for sp in s]
            p_sum = pr[0]
            for pp in pr[1:]:
                p_sum = p_sum + pp
            pv = _dot(pr[0].astype(BF16), vbuf[u, 0].astype(BF16))
            for p in range(1, n_pg):
                pv = pv + _dot(pr[p].astype(BF16), vbuf[u, p].astype(BF16))
            l_scr[...] = alpha * l_scr[...] + jnp.sum(p_sum, axis=-1, keepdims=True)
            acc_scr[...] = alpha * acc_scr[...] + pv
            m_scr[...] = m_new

            if u == n_slots - 1:
                pl.when(jc == chunks_per_seq - 1)(functools.partial(end_sequence, b))
        return carry

    lax.fori_loop(0, n_chunks // n_slots, group, 0)


def _attn_sample(qb, kn_pad, vn_pad, cache_k, cache_v, page_table, lam_vecs, norm_w,
                 layer, lam_init, t_valid):
    bsz, tp, _ = qb.shape
    n_pages = page_table.shape[1]
    slab = cache_k.shape[2]
    page = slab // H_B
    n_pg, n_slots = DEC_PAGES_PER_CHUNK, DEC_SLOTS
    chunks_per_seq = n_pages // n_pg
    assert n_pages % n_pg == 0 and chunks_per_seq % n_slots == 0, (n_pages, n_pg, n_slots)
    n_rows = H_B * 2 * tp
    n_new = kn_pad.shape[1]
    whole = lambda *shape: pl.BlockSpec(shape, lambda i, pt: (0,) * len(shape))
    grid_spec = pltpu.PrefetchScalarGridSpec(
        num_scalar_prefetch=1,
        grid=(1,),
        in_specs=[whole(bsz, tp, W_B), whole(bsz, n_new, DV_B), whole(bsz, n_new, DV_B),
                  pl.BlockSpec((None, 4, DH_B), lambda i, pt: (layer, 0, 0)),
                  pl.BlockSpec((None, 1, DV_B), lambda i, pt: (layer, 0, 0)),
                  pl.BlockSpec(memory_space=pl.ANY), pl.BlockSpec(memory_space=pl.ANY)],
        out_specs=whole(bsz, tp, W_B),
        scratch_shapes=[pltpu.VMEM((n_slots, n_pg, slab, DV_B), F32),
                        pltpu.VMEM((n_slots, n_pg, slab, DV_B), F32),
                        pltpu.SemaphoreType.DMA((2, n_slots, n_pg)),
                        pltpu.VMEM((n_rows, 2 * DH_B), BF16),
                        pltpu.VMEM((n_rows, 1), F32), pltpu.VMEM((n_rows, 1), F32),
                        pltpu.VMEM((n_rows, DV_B), F32)],
    )
    return pl.pallas_call(
        functools.partial(_dec_kernel, layer=layer, n_pg=n_pg, n_slots=n_slots,
                          chunks_per_seq=chunks_per_seq, page=page, past_len=n_pages * page,
                          lam_init=lam_init, t_valid=t_valid),
        grid_spec=grid_spec,
        out_shape=jax.ShapeDtypeStruct((bsz, tp, W_B), F32),
        compiler_params=_cparams(("arbitrary",)),
        name="diff_attn_sample",
    )(page_table, qb, kn_pad, vn_pad, lam_vecs, norm_w, cache_k, cache_v)


def _ffn_kernel(*refs, tm, seq_len, per_row_state):
    (x_ref, oa_ref, ob_ref, wo_ref, g1_ref, b1_ref,
     wup_ref, cw_ref, cb_ref, wdn_ref, g_ref, b_ref) = refs[:12]
    if per_row_state:
        p1_ref, p2_ref, o_ref, a_ref, h_scr = refs[12:]
    else:
        o_ref, cs_ref, h_scr, a_scr, g_scr = refs[12:]
        ti = pl.program_id(1)

        @pl.when(ti == 0)
        def _():
            a_scr[0:8, :] = jnp.zeros((8, D_FF), F32)

    mixed = (_dot(oa_ref[...].astype(BF16), wo_ref[0:W_A, :])
             + _dot(ob_ref[...].astype(BF16), wo_ref[W_A:W_A + W_B, :]))
    x = _layernorm(ALPHA * x_ref[...] + mixed, g1_ref[...], b1_ref[...])
    x16 = x.astype(BF16)
    for c0 in range(0, D_FF, FF_CHUNK):
        cs = slice(c0, c0 + FF_CHUNK)
        a = _dot(x16, wup_ref[:, cs])
        g = _dot(x16, wup_ref[:, D_FF + c0:D_FF + c0 + FF_CHUNK])
        cw = cw_ref[:, cs]
        cb = cb_ref[:, cs]
        if per_row_state:
            t = lax.broadcasted_iota(jnp.int32, (tm, 1), 0) % seq_len
            am1 = jnp.where(t >= 1, pltpu.roll(a, 1, 0), p1_ref[:, cs])
            am2 = jnp.where(t >= 2, pltpu.roll(a, 2, 0), p2_ref[:, cs])
            a_ref[:, cs] = a
            c = cb + am2 * cw[0:1] + am1 * cw[1:2] + a * cw[2:3]
            h_scr[:, cs] = (_silu(c) * g).astype(BF16)
        else:
            a_scr[8:8 + tm, cs] = a
            g_scr[:, cs] = g
            for r0 in range(0, tm, FF_ROWS):
                c = (cb + a_scr[6 + r0:6 + r0 + FF_ROWS, cs] * cw[0:1]
                     + a_scr[7 + r0:7 + r0 + FF_ROWS, cs] * cw[1:2]
                     + a_scr[8 + r0:8 + r0 + FF_ROWS, cs] * cw[2:3])
                h_scr[r0:r0 + FF_ROWS, cs] = (
                    _silu(c) * g_scr[r0:r0 + FF_ROWS, cs]).astype(BF16)
    if not per_row_state:
        @pl.when(ti == pl.num_programs(1) - 1)
        def _():
            cs_ref[...] = a_scr[6 + tm:8 + tm, :]

        a_scr[0:8, :] = a_scr[tm:tm + 8, :]
    y = _dot(h_scr[...], wdn_ref[...])
    o_ref[...] = _layernorm(ALPHA * x + y, g_ref[...], b_ref[...])


def _ffn_specs(layer):
    once = pl.Buffered(1)
    return [pl.BlockSpec((None, D_MODEL, D_MODEL), lambda *a: (layer, 0, 0), pipeline_mode=once),
            pl.BlockSpec((None, 1, D_MODEL), lambda *a: (layer, 0, 0)),
            pl.BlockSpec((None, 1, D_MODEL), lambda *a: (layer, 0, 0)),
            pl.BlockSpec((None, D_MODEL, 2 * D_FF), lambda *a: (layer, 0, 0), pipeline_mode=once),
            pl.BlockSpec((None, CONV_W, D_FF), lambda *a: (layer, 0, 0)),
            pl.BlockSpec((None, 1, D_FF), lambda *a: (layer, 0, 0)),
            pl.BlockSpec((None, D_FF, D_MODEL), lambda *a: (layer, 0, 0), pipeline_mode=once),
            pl.BlockSpec((None, 1, D_MODEL), lambda *a: (layer, 0, 0)),
            pl.BlockSpec((None, 1, D_MODEL), lambda *a: (layer, 0, 0))]


def _ffn_prompt(x, oa, ob, weights, layer):
    bsz, t, _ = x.shape
    tm = min(512, t)
    tile = lambda w: pl.BlockSpec((None, tm, w), lambda b, i: (b, i, 0))
    return pl.pallas_call(
        functools.partial(_ffn_kernel, tm=tm, seq_len=t, per_row_state=False),
        grid=(bsz, t // tm),
        in_specs=[tile(D_MODEL), tile(W_A), tile(W_B)] + _ffn_specs(layer),
        out_specs=[pl.BlockSpec((None, tm, D_MODEL), lambda b, i: (b, i, 0)),
                   pl.BlockSpec((None, CONV_W - 1, D_FF), lambda b, i: (b, 0, 0))],
        out_shape=[jax.ShapeDtypeStruct((bsz, t, D_MODEL), F32),
                   jax.ShapeDtypeStruct((bsz, CONV_W - 1, D_FF), F32)],
        scratch_shapes=[pltpu.VMEM((tm, D_FF), BF16), pltpu.VMEM((8 + tm, D_FF), F32),
                        pltpu.VMEM((tm, D_FF), F32)],
        compiler_params=_cparams(("parallel", "arbitrary")),
        name="convffn_prompt",
    )(x, oa, ob, *weights)


def _ffn_sample(x, oa, ob, weights, p1, p2, layer, seq_len):
    n = x.shape[0]
    full = lambda i: (0, 0)
    whole = lambda w: pl.BlockSpec((n, w), full)
    return pl.pallas_call(
        functools.partial(_ffn_kernel, tm=n, seq_len=seq_len, per_row_state=True),
        grid=(1,),
        in_specs=[whole(D_MODEL), whole(W_A), whole(W_B)] + _ffn_specs(layer)
                 + [whole(D_FF), whole(D_FF)],
        out_specs=[pl.BlockSpec((n, D_MODEL), full), pl.BlockSpec((n, D_FF), full)],
        out_shape=[jax.ShapeDtypeStruct((n, D_MODEL), F32),
                   jax.ShapeDtypeStruct((n, D_FF), F32)],
        scratch_shapes=[pltpu.VMEM((n, D_FF), BF16)],
        compiler_params=_cparams(("arbitrary",)),
        name="convffn_sample",
    )(x, oa, ob, *weights, p1, p2)


def kernel(x_prompt, x_sample, cache_k, cache_v, state_hgrn, state_ffn_conv, page_table,
           ln_in_g, ln_in_b, w_in, hgrn_lb_logits, hgrn_norm_w,
           lambda_q1, lambda_k1, lambda_q2, lambda_k2, diff_norm_w, w_o,
           ln1_g, ln1_b, w_up, conv_w, conv_b, w_down, ln2_g, ln2_b):
    bp, tp_, _ = x_prompt.shape
    bs, ts, _ = x_sample.shape
    pad_t = SAMPLE_PAD_T
    n_phys, page = cache_k.shape[1], cache_k.shape[2]
    slab = page * H_B
    np_rows, ns_rows = bp * tp_, bs * pad_t

    w_in16 = w_in.astype(BF16)
    w_o16 = w_o.astype(BF16)
    w_up16 = w_up.astype(BF16)
    w_dn16 = w_down.astype(BF16)
    lam_vecs = jnp.stack([lambda_q1, lambda_k1, lambda_q2, lambda_k2], axis=1)
    ck = cache_k.reshape(DEPTH, n_phys, slab, 2 * DH_B)
    cv = cache_v.reshape(DEPTH, n_phys, slab, DV_B)
    s_zero = jnp.zeros((bp, H_A, DK_A, DV_A), F32)
    per_layer = lambda a: a.reshape(DEPTH, 1, a.shape[-1])
    hgrn_norm_w, diff_norm_w = per_layer(hgrn_norm_w), per_layer(diff_norm_w)
    ffn_weights = (w_o16, per_layer(ln1_g), per_layer(ln1_b), w_up16, conv_w, per_layer(conv_b),
                   w_dn16, per_layer(ln2_g), per_layer(ln2_b))

    xp = _ln_in(x_prompt.reshape(np_rows, D_MODEL), ln_in_g, ln_in_b)
    xs = _ln_in(jnp.pad(x_sample, ((0, 0), (0, pad_t - ts), (0, 0))).reshape(ns_rows, D_MODEL),
                ln_in_g, ln_in_b)

    k_s, v_s, s_p, s_s, c_p, c_s = [], [], [], [], [], []
    kv_flat = [jnp.zeros((DEPTH, np_rows * H_B, DV_B), F32) for _ in range(2)]
    for l in range(DEPTH):
        lam_init = 0.8 - 0.6 * math.exp(-0.3 * l)

        qa, logf, kc, ia, sg, qb, kb16, vb16, *kv_flat = _inproj(
            xp, w_in16, hgrn_lb_logits, l, kv_flat)
        seq = lambda a: a.reshape(bp, tp_, a.shape[-1])
        oa, sp = _hgrn(seq(qa), seq(logf), seq(kc), seq(ia), seq(sg), s_zero, hgrn_norm_w, l, tp_)
        ob = _attn_prompt(seq(qb), seq(kb16), seq(vb16), lam_vecs, diff_norm_w, l, lam_init)
        x2, cp = _ffn_prompt(seq(xp), oa, ob, ffn_weights, l)
        xp = x2.reshape(np_rows, D_MODEL)
        s_p.append(sp)
        c_p.append(cp)

        qa, logf, kc, ia, sg, qb, kb, vb = _inproj(xs, w_in16, hgrn_lb_logits, l)
        seq = lambda a: a.reshape(bs, pad_t, a.shape[-1])
        oa, ss = _hgrn(seq(qa), seq(logf), seq(kc), seq(ia), seq(sg), state_hgrn[l],
                       hgrn_norm_w, l, ts)
        grow = lambda a: jnp.pad(a.reshape(bs, pad_t * H_B, DV_B),
                                 ((0, 0), (0, DEC_NEW_ROWS - pad_t * H_B), (0, 0)))
        ob = _attn_sample(seq(qb), grow(kb), grow(vb), ck, cv, page_table, lam_vecs,
                          diff_norm_w, l, lam_init, ts)
        conv0 = state_ffn_conv[l]
        p1 = jnp.pad(conv0[:, 1:2], ((0, 0), (0, pad_t - 1), (0, 0))).reshape(ns_rows, D_FF)
        p2 = jnp.pad(conv0, ((0, 0), (0, pad_t - 2), (0, 0))).reshape(ns_rows, D_FF)
        xs, a_s = _ffn_sample(xs, oa.reshape(ns_rows, W_A), ob.reshape(ns_rows, W_B),
                              ffn_weights, p1, p2, l, pad_t)
        k_s.append(seq(kb)[:, :ts].reshape(bs, ts, H_B, 2 * DH_B))
        v_s.append(seq(vb)[:, :ts].reshape(bs, ts, H_B, DV_B))
        s_s.append(ss)
        c_s.append(a_s.reshape(bs, pad_t, D_FF)[:, ts - (CONV_W - 1):ts])

    y_p = xp.reshape(bp, tp_, D_MODEL)
    y_s = xs.reshape(bs, pad_t, D_MODEL)[:, :ts]
    k_p, v_p = (a.reshape(DEPTH, bp, tp_, H_B, DV_B) for a in kv_flat)
    return (y_p, y_s, k_p, v_p, jnp.stack(k_s), jnp.stack(v_s),
            jnp.stack(s_p), jnp.stack(s_s), jnp.stack(c_p), jnp.stack(c_s))
```

```python
import functools
import math

import jax
import jax.numpy as jnp
from jax import lax
from jax.experimental import pallas as pl
from jax.experimental.pallas import tpu as pltpu

F32 = jnp.float32
BF16 = jnp.bfloat16

D_MODEL = 1024
DEPTH = 4
H_A, DK_A, DV_A = 4, 128, 128
H_B, DH_B, DV_B = 4, 64, 128
W_A = H_A * DV_A
W_B = H_B * DV_B
IN_COLS = 2 * H_A * DK_A + 2 * W_A + 4 * H_B * DH_B + W_B
D_FF = 2816
FF_CHUNK = 256
FF_ROWS = 64
CONV_W = 3
CHUNK = 128
HGRN_SEQS_PER_STEP = 4
ALPHA = (2 * DEPTH) ** 0.25
LN_EPS = 1e-5
RMS_EPS = 1e-6
LOG2_E = 1.0 / math.log(2.0)
NEG = -1e30
LANES = 128
SAMPLE_PAD_T = 8
DEC_PAGES_PER_CHUNK = 8
DEC_SLOTS = 4
DEC_NEW_ROWS = 128
VMEM_LIMIT = 56 * 1024 * 1024


def _cparams(sem):
    return pltpu.CompilerParams(dimension_semantics=sem, vmem_limit_bytes=VMEM_LIMIT)


def _layernorm(x, g, b):
    mu = jnp.mean(x, axis=-1, keepdims=True)
    xc = x - mu
    var = jnp.mean(xc * xc, axis=-1, keepdims=True)
    return xc * lax.rsqrt(var + LN_EPS) * g + b


def _silu(x):
    return x / (1.0 + jnp.exp2(x * (-LOG2_E)))


def _dot(a, b):
    return jnp.dot(a, b, preferred_element_type=F32)


def _dot_nt(a, b):
    return lax.dot_general(a, b, (((1,), (1,)), ((), ())), preferred_element_type=F32)


def _dot_tn(a, b):
    return lax.dot_general(a, b, (((0,), (0,)), ((), ())), preferred_element_type=F32)


def _lambda(lam_ref, lam_init):
    lv = lam_ref[...]
    p1 = jnp.sum(lv[0:1] * lv[1:2], axis=-1, keepdims=True)
    p2 = jnp.sum(lv[2:3] * lv[3:4], axis=-1, keepdims=True)
    return jnp.exp(p1) - jnp.exp(p2) + lam_init


def _diff_norm(acc1, l1, acc2, l2, lam, nw, lam_init):
    o = acc1 / l1 - lam * (acc2 / l2)
    ms = jnp.mean(o * o, axis=-1, keepdims=True)
    return o * lax.rsqrt(ms + RMS_EPS) * nw * (1.0 - lam_init)


def _ln_kernel(x_ref, g_ref, b_ref, o_ref):
    o_ref[...] = _layernorm(x_ref[...], g_ref[...], b_ref[...])


def _ln_in(x, g, b):
    n = x.shape[0]
    tm = min(512, n)
    return pl.pallas_call(
        _ln_kernel,
        grid=(n // tm,),
        in_specs=[pl.BlockSpec((tm, D_MODEL), lambda i: (i, 0)),
                  pl.BlockSpec((1, D_MODEL), lambda i: (0, 0)),
                  pl.BlockSpec((1, D_MODEL), lambda i: (0, 0))],
        out_specs=pl.BlockSpec((tm, D_MODEL), lambda i: (i, 0)),
        out_shape=jax.ShapeDtypeStruct((n, D_MODEL), F32),
        compiler_params=_cparams(("parallel",)),
        name="ln_in",
    )(x, g.reshape(1, D_MODEL), b.reshape(1, D_MODEL))


def _inproj_kernel(*refs, layer, flat_kv):
    if flat_kv:
        x_ref, w_ref, lbl_ref, _, _ = refs[:5]
        (qa_ref, logf_ref, kc_ref, ia_ref, sg_ref, qb_ref,
         kb16_ref, vb16_ref, kflat_ref, vflat_ref) = refs[-10:]
    else:
        (x_ref, w_ref, lbl_ref, qa_ref, logf_ref, kc_ref, ia_ref, sg_ref,
         qb_ref, kb_ref, vb_ref) = refs
    x = x_ref[...].astype(BF16)
    tm = x.shape[0]

    def proj(c0, n):
        return _dot(x, w_ref[:, c0:c0 + n])

    qa = proj(0, W_A)
    qa_ref[...] = (_silu(qa) * (DK_A ** -0.5)).astype(BF16)

    fa = proj(W_A, W_A)
    log_sig = jnp.minimum(fa, 0.0) - jnp.log1p(jnp.exp(-jnp.abs(fa)))
    sig_neg = 1.0 / (1.0 + jnp.exp(fa))
    if layer == 0:
        logf_ref[...] = log_sig
        kc_ref[...] = sig_neg.astype(BF16)
    else:
        lg = lbl_ref[...]
        e = jnp.exp(lg - jnp.max(lg, axis=0, keepdims=True))
        sm = e / jnp.sum(e, axis=0, keepdims=True)
        lb = sm[1:2]
        for j in range(2, layer + 1):
            lb = lb + sm[j:j + 1]
        u = jnp.log(lb)
        w = jnp.log1p(-lb) + log_sig
        logf_ref[...] = jnp.maximum(u, w) + jnp.log1p(jnp.exp(-jnp.abs(u - w)))
        kc_ref[...] = ((1.0 - lb) * sig_neg).astype(BF16)

    ia_ref[...] = proj(2 * W_A, W_A).astype(BF16)
    sg_ref[...] = _silu(proj(3 * W_A, W_A)).astype(BF16)
    q_scale = DH_B ** -0.5 * (LOG2_E if flat_kv else 1.0)
    qb_ref[...] = (proj(4 * W_A, W_B) * q_scale).astype(qb_ref.dtype)
    kb = proj(4 * W_A + W_B, W_B)
    vb = proj(4 * W_A + 2 * W_B, W_B)
    if flat_kv:
        kb16_ref[...] = kb.astype(BF16)
        vb16_ref[...] = vb.astype(BF16)
        for h in range(H_B):
            hs = slice(h * DV_B, (h + 1) * DV_B)
            kflat_ref[pl.ds(h, tm, stride=H_B), :] = kb[:, hs]
            vflat_ref[pl.ds(h, tm, stride=H_B), :] = vb[:, hs]
    else:
        kb_ref[...] = kb
        vb_ref[...] = vb


def _inproj(x, w_in16, lb_logits, layer, kv_flat=None):
    n = x.shape[0]
    tm = min(512, n)
    row = lambda i: (i, 0)
    f32_out = jax.ShapeDtypeStruct((n, W_A), F32)
    bf_out = jax.ShapeDtypeStruct((n, W_A), BF16)
    blk = pl.BlockSpec((tm, W_A), row)
    in_specs = [pl.BlockSpec((tm, D_MODEL), row),
                pl.BlockSpec((None, D_MODEL, IN_COLS), lambda i: (layer, 0, 0),
                             pipeline_mode=pl.Buffered(1)),
                pl.BlockSpec((DEPTH, W_A), lambda i: (0, 0))]
    if kv_flat is None:
        out_specs = [blk] * 8
        out_shape = [bf_out, f32_out, bf_out, bf_out, bf_out] + [f32_out] * 3
        args, aliases = (), {}
    else:
        flat = jax.ShapeDtypeStruct((DEPTH, n * H_B, DV_B), F32)
        out_specs = [blk] * 8 + [pl.BlockSpec((None, tm * H_B, DV_B), lambda i: (layer, i, 0))] * 2
        out_shape = [bf_out, f32_out] + [bf_out] * 6 + [flat, flat]
        args = tuple(kv_flat)
        in_specs += [pl.BlockSpec(memory_space=pl.ANY)] * 2
        aliases = {3: 8, 4: 9}
    return pl.pallas_call(
        functools.partial(_inproj_kernel, layer=layer, flat_kv=kv_flat is not None),
        grid=(n // tm,),
        in_specs=in_specs,
        out_specs=out_specs,
        out_shape=out_shape,
        input_output_aliases=aliases,
        compiler_params=_cparams(("parallel",)),
        name="inproj",
    )(x, w_in16, lb_logits, *args)


def _boundary_rows(b, m, rows):
    c, w = b.shape
    if 2 * m >= 8:
        parts = [jnp.broadcast_to(b[p * 2 * m + m - 1:p * 2 * m + m], (2 * m, w))
                 for p in range(c // (2 * m))]
        return parts[0] if len(parts) == 1 else jnp.concatenate(parts, axis=0)
    if m == 2:
        r = rows % 4
        return jnp.where(r == 0, pltpu.roll(b, c - 1, 0),
                         jnp.where(r == 1, b,
                                   jnp.where(r == 2, pltpu.roll(b, 1, 0), pltpu.roll(b, 2, 0))))
    return jnp.where(rows % 2 == 0, b, pltpu.roll(b, 1, 0))


def _hgrn_kernel(q_ref, g_ref, kc_ref, v_ref, sg_ref, s0_ref, nw_ref, o_ref, s_ref, st_scr,
                 *, chunk, t_valid, t_total):
    ci = pl.program_id(1)
    n_seq = q_ref.shape[0]

    @pl.when(ci == 0)
    def _():
        for sq in range(n_seq):
            st_scr[sq] = jnp.concatenate([s0_ref[sq, h].T for h in range(H_A)], axis=1)

    for sq in range(n_seq):
        _hgrn_chunk(sq, ci, q_ref, g_ref, kc_ref, v_ref, sg_ref, nw_ref, o_ref, st_scr,
                    chunk=chunk, t_valid=t_valid, t_total=t_total)

    @pl.when(ci == pl.num_programs(1) - 1)
    def _():
        for sq in range(n_seq):
            for h in range(H_A):
                s_ref[sq, h] = st_scr[sq, :, h * DK_A:(h + 1) * DK_A].T


def _hgrn_chunk(sq, ci, q_ref, g_ref, kc_ref, v_ref, sg_ref, nw_ref, o_ref, st_scr,
                *, chunk, t_valid, t_total):
    rows = lax.broadcasted_iota(jnp.int32, (chunk, 1), 0)
    t_idx = lax.broadcasted_iota(jnp.int32, (chunk, chunk), 0)
    s_idx = lax.broadcasted_iota(jnp.int32, (chunk, chunk), 1)
    tri = (s_idx <= t_idx).astype(F32)

    q = q_ref[sq].astype(F32)
    g = g_ref[sq]
    kc = kc_ref[sq].astype(F32)
    if t_valid < t_total:
        valid = (ci * chunk + rows) < t_valid
        g = jnp.where(valid, g, 0.0)
        kc = jnp.where(valid, kc, 0.0)
    v16 = v_ref[sq]
    heads = [slice(h * DK_A, (h + 1) * DK_A) for h in range(H_A)]

    b = jnp.dot(tri, g, preferred_element_type=F32, precision=lax.Precision.HIGHEST) * LOG2_E

    level = jnp.where(s_idx <= t_idx, 31 - lax.clz(t_idx ^ s_idx), -2)
    q16 = q.astype(BF16)
    k16 = kc.astype(BF16)
    a = [jnp.where(level == -1, _dot_nt(q16[:, hs], k16[:, hs]), 0.0) for hs in heads]
    m = chunk // 2
    while m >= 1:
        r = _boundary_rows(b, m, rows)
        second_half = (rows // m) % 2 == 1
        x = (jnp.where(second_half, q, kc) * jnp.exp2(-jnp.abs(b - r))).astype(BF16)
        keep = level == (m.bit_length() - 1)
        a = [jnp.where(keep, _dot_nt(x[:, hs], x[:, hs]), a[h]) for h, hs in enumerate(heads)]
        m //= 2

    qe = (q * jnp.exp2(b)).astype(BF16)
    b_last = b[chunk - 1:chunk]
    decay = jnp.exp2(b_last)
    k_last = (kc * jnp.exp2(b_last - b)).astype(BF16)
    st = st_scr[sq]
    st16 = st.astype(BF16)
    o = [_dot_nt(qe[:, hs], st16[:, hs]) + _dot(a[h].astype(BF16), v16[:, hs])
         for h, hs in enumerate(heads)]
    st_new = st * decay + jnp.concatenate(
        [_dot_tn(v16[:, hs], k_last[:, hs]) for hs in heads], axis=1)
    st_scr[sq] = st_new
    nw = nw_ref[...]
    scale = jnp.concatenate(
        [jnp.broadcast_to(lax.rsqrt(jnp.mean(oh * oh, axis=-1, keepdims=True) + RMS_EPS),
                          (chunk, DV_A)) * nw for oh in o], axis=1)
    o_ref[sq] = (jnp.concatenate(o, axis=1) * scale * sg_ref[sq].astype(F32)).astype(o_ref.dtype)


def _hgrn(qa, logf, kc, ia, sg, s0, s0_layer, norm_w, layer, t_valid):
    bsz, t, _ = qa.shape
    chunk = CHUNK if t % CHUNK == 0 else t
    n_seq = math.gcd(bsz, HGRN_SEQS_PER_STEP if chunk == CHUNK else 2 * HGRN_SEQS_PER_STEP)
    seq = pl.BlockSpec((n_seq, chunk, W_A), lambda b, c: (b, c, 0))
    state = pl.BlockSpec((n_seq, H_A, DK_A, DV_A), lambda b, c: (b, 0, 0, 0))
    return pl.pallas_call(
        functools.partial(_hgrn_kernel, chunk=chunk, t_valid=t_valid, t_total=t),
        grid=(bsz // n_seq, t // chunk),
        in_specs=[seq, seq, seq, seq, seq,
                  pl.BlockSpec((None, n_seq, H_A, DK_A, DV_A), lambda b, c: (s0_layer, b, 0, 0, 0)),
                  pl.BlockSpec((None, 1, DV_A), lambda b, c: (layer, 0, 0))],
        out_specs=[seq, state],
        out_shape=[jax.ShapeDtypeStruct((bsz, t, W_A), BF16),
                   jax.ShapeDtypeStruct((bsz, H_A, DK_A, DV_A), F32)],
        scratch_shapes=[pltpu.VMEM((n_seq, DV_A, H_A * DK_A), F32)],
        compiler_params=_cparams(("parallel", "arbitrary")),
        name="hgrn2",
    )(qa, logf, kc, ia, sg, s0, norm_w)


def _head_slope(h):
    return jnp.where(h == 0, 2.0 ** -2, jnp.where(h == 1, 2.0 ** -4,
                     jnp.where(h == 2, 2.0 ** -6, 2.0 ** -8))).astype(F32)


def _fold_lanes(x, op):
    out = x[:, 0:LANES]
    for c in range(LANES, x.shape[1], LANES):
        out = op(out, x[:, c:c + LANES])
    return out


def _attn_kernel(q_ref, k_ref, v_ref, lam_ref, nw_ref, o_ref, *, tq, n_q, lam_init):
    h = pl.program_id(1)
    slope = _head_slope(h) * LOG2_E
    lam = _lambda(lam_ref, lam_init)
    nw = nw_ref[...]
    lane = lax.broadcasted_iota(jnp.int32, (tq, 2 * DH_B), 1)
    k_diag = lax.broadcasted_iota(jnp.int32, (1, tq), 1)
    causal = k_diag <= lax.broadcasted_iota(jnp.int32, (2 * tq, 1), 0) % tq
    for i in range(n_q):
        qf = q_ref[i * tq:(i + 1) * tq, :].astype(F32)
        q2 = jnp.concatenate([jnp.where(lane < DH_B, qf, 0.0),
                              jnp.where(lane >= DH_B, qf, 0.0)], axis=0).astype(BF16)
        s_d = _dot_nt(q2, k_ref[i * tq:(i + 1) * tq, :]) + slope * k_diag.astype(F32)
        s_d = jnp.where(causal, s_d, NEG)
        m = _fold_lanes(s_d, jnp.maximum)
        if i > 0:
            k_off = lax.broadcasted_iota(jnp.int32, (1, i * tq), 1) - i * tq
            s_o = _dot_nt(q2, k_ref[0:i * tq, :]) + slope * k_off.astype(F32)
            m = jnp.maximum(m, _fold_lanes(s_o, jnp.maximum))
        m = jnp.max(m, axis=-1, keepdims=True)
        p_d = jnp.exp2(s_d - m)
        l = _fold_lanes(p_d, jnp.add)
        acc = _dot(p_d.astype(BF16), v_ref[i * tq:(i + 1) * tq, :])
        if i > 0:
            p_o = jnp.exp2(s_o - m)
            l = l + _fold_lanes(p_o, jnp.add)
            acc = acc + _dot(p_o.astype(BF16), v_ref[0:i * tq, :])
        l = jnp.sum(l, axis=-1, keepdims=True)
        o = _diff_norm(acc[0:tq], l[0:tq], acc[tq:2 * tq], l[tq:2 * tq], lam, nw, lam_init)
        o_ref[i * tq:(i + 1) * tq, :] = o.astype(o_ref.dtype)


def _attn_prompt(qb16, kb16, vb16, lam_vecs, norm_w, layer, lam_init):
    bsz, t, _ = qb16.shape
    tq = min(256, t)
    seq_head = pl.BlockSpec((None, t, DV_B), lambda b, h: (b, 0, h))
    return pl.pallas_call(
        functools.partial(_attn_kernel, tq=tq, n_q=t // tq, lam_init=lam_init),
        grid=(bsz, H_B),
        in_specs=[seq_head, seq_head, seq_head,
                  pl.BlockSpec((None, 4, DH_B), lambda b, h: (layer, 0, 0)),
                  pl.BlockSpec((None, 1, DV_B), lambda b, h: (layer, 0, 0))],
        out_specs=seq_head,
        out_shape=jax.ShapeDtypeStruct((bsz, t, W_B), BF16),
        compiler_params=_cparams(("parallel", "parallel")),
        name="diff_attn_prompt",
    )(qb16, kb16, vb16, lam_vecs, norm_w)


def _softmax_update(s, v16, m_scr, l_scr, acc_scr):
    m_prev = m_scr[...]
    m_new = jnp.maximum(m_prev, jnp.max(s, axis=-1, keepdims=True))
    alpha = jnp.exp(m_prev - m_new)
    p = jnp.exp(s - m_new)
    l_scr[...] = alpha * l_scr[...] + jnp.sum(p, axis=-1, keepdims=True)
    acc_scr[...] = alpha * acc_scr[...] + _dot(p.astype(BF16), v16)
    m_scr[...] = m_new


def _dec_kernel(pt_ref, q_ref, kn_ref, vn_ref, lam_ref, nw_ref, ck_hbm, cv_hbm, o_ref,
                kbuf, vbuf, sem, qx_scr, m_scr, l_scr, acc_scr,
                *, layer, n_pg, n_slots, chunks_per_seq, page, past_len, lam_init, t_valid):
    tp = SAMPLE_PAD_T
    slab = page * H_B
    n_rows = H_B * 2 * tp
    n_chunks = q_ref.shape[0] * chunks_per_seq
    row = lax.broadcasted_iota(jnp.int32, (n_rows, 1), 0)
    row_head = row // (2 * tp)
    slope = jnp.where(row_head == 0, 2.0 ** -2, jnp.where(row_head == 1, 2.0 ** -4,
                      jnp.where(row_head == 2, 2.0 ** -6, 2.0 ** -8))).astype(F32)
    col = lax.broadcasted_iota(jnp.int32, (1, slab), 1)
    own_head = col % H_B == row_head
    lam = _lambda(lam_ref, lam_init)
    nw = nw_ref[...]

    def page_copies(c, slot):
        b = c // chunks_per_seq
        first = (c % chunks_per_seq) * n_pg
        out = []
        for p in range(n_pg):
            pid = pt_ref[b, first + p]
            out.append(pltpu.make_async_copy(ck_hbm.at[layer, pid], kbuf.at[slot, p], sem.at[0, slot, p]))
            out.append(pltpu.make_async_copy(cv_hbm.at[layer, pid], vbuf.at[slot, p], sem.at[1, slot, p]))
        return out

    def start(c, slot):
        for cp in page_copies(c, slot):
            cp.start()

    def begin_sequence(b):
        qf = q_ref[b]
        lane = lax.broadcasted_iota(jnp.int32, (tp, 2 * DH_B), 1)
        parts = []
        for h in range(H_B):
            qh = qf[:, h * 2 * DH_B:(h + 1) * 2 * DH_B]
            parts.append(jnp.where(lane < DH_B, qh, 0.0))
            parts.append(jnp.where(lane >= DH_B, qh, 0.0))
        qx_scr[...] = jnp.concatenate(parts, axis=0).astype(BF16)
        m_scr[...] = jnp.full(m_scr.shape, NEG, F32)
        l_scr[...] = jnp.zeros(l_scr.shape, F32)
        acc_scr[...] = jnp.zeros(acc_scr.shape, F32)

    def end_sequence(b):
        col_n = lax.broadcasted_iota(jnp.int32, (1, kn_ref.shape[1]), 1)
        t_k = col_n // H_B
        s_n = _dot_nt(qx_scr[...], kn_ref[b].astype(BF16))
        ok = (col_n % H_B == row_head) & (t_k <= row % tp) & (t_k < t_valid)
        s_n = jnp.where(ok, s_n + slope * t_k.astype(F32), NEG)
        _softmax_update(s_n, vn_ref[b].astype(BF16), m_scr, l_scr, acc_scr)
        outs = []
        for h in range(H_B):
            r0 = h * 2 * tp
            outs.append(_diff_norm(acc_scr[r0:r0 + tp], l_scr[r0:r0 + tp],
                                   acc_scr[r0 + tp:r0 + 2 * tp], l_scr[r0 + tp:r0 + 2 * tp],
                                   lam, nw, lam_init))
        o_ref[b] = jnp.concatenate(outs, axis=1).astype(o_ref.dtype)

    for c in range(n_slots - 1):
        start(c, c)

    def group(it, carry):
        for u in range(n_slots):
            c = it * n_slots + u
            b = c // chunks_per_seq
            jc = c % chunks_per_seq

            @pl.when(c + n_slots - 1 < n_chunks)
            def _():
                start(c + n_slots - 1, (u + n_slots - 1) % n_slots)

            if u == 0:
                pl.when(jc == 0)(functools.partial(begin_sequence, b))

            for cp in page_copies(c, u):
                cp.wait()

            qx = qx_scr[...]
            pos0 = col // H_B + (jc * (n_pg * page) - past_len)
            s = [jnp.where(own_head,
                           _dot_nt(qx, kbuf[u, p].astype(BF16))
                           + slope * (pos0 + p * page).astype(F32), NEG) for p in range(n_pg)]
            s_max = s[0]
            for sp in s[1:]:
                s_max = jnp.maximum(s_max, sp)
            m_prev = m_scr[...]
            m_new = jnp.maximum(m_prev, jnp.max(s_max, axis=-1, keepdims=True))
            alpha = jnp.exp(m_prev - m_new)
            pr = [jnp.exp(sp - m_new) for sp in s]
            p_sum = pr[0]
            for pp in pr[1:]:
                p_sum = p_sum + pp
            pv = _dot(pr[0].astype(BF16), vbuf[u, 0].astype(BF16))
            for p in range(1, n_pg):
                pv = pv + _dot(pr[p].astype(BF16), vbuf[u, p].astype(BF16))
            l_scr[...] = alpha * l_scr[...] + jnp.sum(p_sum, axis=-1, keepdims=True)
            acc_scr[...] = alpha * acc_scr[...] + pv
            m_scr[...] = m_new

            if u == n_slots - 1:
                pl.when(jc == chunks_per_seq - 1)(functools.partial(end_sequence, b))
        return carry

    lax.fori_loop(0, n_chunks // n_slots, group, 0)


def _attn_sample(qb, kn_pad, vn_pad, cache_k, cache_v, page_table, lam_vecs, norm_w,
                 layer, lam_init, t_valid):
    bsz, tp, _ = qb.shape
    n_pages = page_table.shape[1]
    slab = cache_k.shape[2]
    page = slab // H_B
    n_pg, n_slots = DEC_PAGES_PER_CHUNK, DEC_SLOTS
    chunks_per_seq = n_pages // n_pg
    assert n_pages % n_pg == 0 and chunks_per_seq % n_slots == 0, (n_pages, n_pg, n_slots)
    n_rows = H_B * 2 * tp
    n_new = kn_pad.shape[1]
    whole = lambda *shape: pl.BlockSpec(shape, lambda i, pt: (0,) * len(shape))
    grid_spec = pltpu.PrefetchScalarGridSpec(
        num_scalar_prefetch=1,
        grid=(1,),
        in_specs=[whole(bsz, tp, W_B), whole(bsz, n_new, DV_B), whole(bsz, n_new, DV_B),
                  pl.BlockSpec((None, 4, DH_B), lambda i, pt: (layer, 0, 0)),
                  pl.BlockSpec((None, 1, DV_B), lambda i, pt: (layer, 0, 0)),
                  pl.BlockSpec(memory_space=pl.ANY), pl.BlockSpec(memory_space=pl.ANY)],
        out_specs=whole(bsz, tp, W_B),
        scratch_shapes=[pltpu.VMEM((n_slots, n_pg, slab, DV_B), F32),
                        pltpu.VMEM((n_slots, n_pg, slab, DV_B), F32),
                        pltpu.SemaphoreType.DMA((2, n_slots, n_pg)),
                        pltpu.VMEM((n_rows, 2 * DH_B), BF16),
                        pltpu.VMEM((n_rows, 1), F32), pltpu.VMEM((n_rows, 1), F32),
                        pltpu.VMEM((n_rows, DV_B), F32)],
    )
    return pl.pallas_call(
        functools.partial(_dec_kernel, layer=layer, n_pg=n_pg, n_slots=n_slots,
                          chunks_per_seq=chunks_per_seq, page=page, past_len=n_pages * page,
                          lam_init=lam_init, t_valid=t_valid),
        grid_spec=grid_spec,
        out_shape=jax.ShapeDtypeStruct((bsz, tp, W_B), F32),
        compiler_params=_cparams(("arbitrary",)),
        name="diff_attn_sample",
    )(page_table, qb, kn_pad, vn_pad, lam_vecs, norm_w, cache_k, cache_v)


def _ffn_kernel(*refs, tm, seq_len, per_row_state):
    (x_ref, oa_ref, ob_ref, wo_ref, g1_ref, b1_ref,
     wup_ref, cw_ref, cb_ref, wdn_ref, g_ref, b_ref) = refs[:12]
    if per_row_state:
        p1_ref, p2_ref, o_ref, a_ref, h_scr = refs[12:]
    else:
        o_ref, cs_ref, h_scr, a_scr, g_scr = refs[12:]
        ti = pl.program_id(1)

        @pl.when(ti == 0)
        def _():
            a_scr[0:8, :] = jnp.zeros((8, D_FF), F32)

    mixed = (_dot(oa_ref[...].astype(BF16), wo_ref[0:W_A, :])
             + _dot(ob_ref[...].astype(BF16), wo_ref[W_A:W_A + W_B, :]))
    x = _layernorm(ALPHA * x_ref[...] + mixed, g1_ref[...], b1_ref[...])
    x16 = x.astype(BF16)
    for c0 in range(0, D_FF, FF_CHUNK):
        cs = slice(c0, c0 + FF_CHUNK)
        a = _dot(x16, wup_ref[:, cs])
        g = _dot(x16, wup_ref[:, D_FF + c0:D_FF + c0 + FF_CHUNK])
        cw = cw_ref[:, cs]
        cb = cb_ref[:, cs]
        if per_row_state:
            t = lax.broadcasted_iota(jnp.int32, (tm, 1), 0) % seq_len
            am1 = jnp.where(t >= 1, pltpu.roll(a, 1, 0), p1_ref[:, cs])
            am2 = jnp.where(t >= 2, pltpu.roll(a, 2, 0), p2_ref[:, cs])
            a_ref[:, cs] = a
            c = cb + am2 * cw[0:1] + am1 * cw[1:2] + a * cw[2:3]
            h_scr[:, cs] = (_silu(c) * g).astype(BF16)
        else:
            a_scr[8:8 + tm, cs] = a
            g_scr[:, cs] = g
            for r0 in range(0, tm, FF_ROWS):
                c = (cb + a_scr[6 + r0:6 + r0 + FF_ROWS, cs] * cw[0:1]
                     + a_scr[7 + r0:7 + r0 + FF_ROWS, cs] * cw[1:2]
                     + a_scr[8 + r0:8 + r0 + FF_ROWS, cs] * cw[2:3])
                h_scr[r0:r0 + FF_ROWS, cs] = (
                    _silu(c) * g_scr[r0:r0 + FF_ROWS, cs]).astype(BF16)
    if not per_row_state:
        @pl.when(ti == pl.num_programs(1) - 1)
        def _():
            cs_ref[...] = a_scr[6 + tm:8 + tm, :]

        a_scr[0:8, :] = a_scr[tm:tm + 8, :]
    y = _dot(h_scr[...], wdn_ref[...])
    o_ref[...] = _layernorm(ALPHA * x + y, g_ref[...], b_ref[...])


def _ffn_specs(layer):
    once = pl.Buffered(1)
    return [pl.BlockSpec((None, D_MODEL, D_MODEL), lambda *a: (layer, 0, 0), pipeline_mode=once),
            pl.BlockSpec((None, 1, D_MODEL), lambda *a: (layer, 0, 0)),
            pl.BlockSpec((None, 1, D_MODEL), lambda *a: (layer, 0, 0)),
            pl.BlockSpec((None, D_MODEL, 2 * D_FF), lambda *a: (layer, 0, 0), pipeline_mode=once),
            pl.BlockSpec((None, CONV_W, D_FF), lambda *a: (layer, 0, 0)),
            pl.BlockSpec((None, 1, D_FF), lambda *a: (layer, 0, 0)),
            pl.BlockSpec((None, D_FF, D_MODEL), lambda *a: (layer, 0, 0), pipeline_mode=once),
            pl.BlockSpec((None, 1, D_MODEL), lambda *a: (layer, 0, 0)),
            pl.BlockSpec((None, 1, D_MODEL), lambda *a: (layer, 0, 0))]


def _ffn_prompt(x, oa, ob, weights, layer):
    bsz, t, _ = x.shape
    tm = min(512, t)
    tile = lambda w: pl.BlockSpec((None, tm, w), lambda b, i: (b, i, 0))
    return pl.pallas_call(
        functools.partial(_ffn_kernel, tm=tm, seq_len=t, per_row_state=False),
        grid=(bsz, t // tm),
        in_specs=[tile(D_MODEL), tile(W_A), tile(W_B)] + _ffn_specs(layer),
        out_specs=[pl.BlockSpec((None, tm, D_MODEL), lambda b, i: (b, i, 0)),
                   pl.BlockSpec((None, CONV_W - 1, D_FF), lambda b, i: (b, 0, 0))],
        out_shape=[jax.ShapeDtypeStruct((bsz, t, D_MODEL), F32),
                   jax.ShapeDtypeStruct((bsz, CONV_W - 1, D_FF), F32)],
        scratch_shapes=[pltpu.VMEM((tm, D_FF), BF16), pltpu.VMEM((8 + tm, D_FF), F32),
                        pltpu.VMEM((tm, D_FF), F32)],
        compiler_params=_cparams(("parallel", "arbitrary")),
        name="convffn_prompt",
    )(x, oa, ob, *weights)


def _ffn_sample(x, oa, ob, weights, p1, p2, layer, seq_len):
    n = x.shape[0]
    full = lambda i: (0, 0)
    whole = lambda w: pl.BlockSpec((n, w), full)
    return pl.pallas_call(
        functools.partial(_ffn_kernel, tm=n, seq_len=seq_len, per_row_state=True),
        grid=(1,),
        in_specs=[whole(D_MODEL), whole(W_A), whole(W_B)] + _ffn_specs(layer)
                 + [whole(D_FF), whole(D_FF)],
        out_specs=[pl.BlockSpec((n, D_MODEL), full), pl.BlockSpec((n, D_FF), full)],
        out_shape=[jax.ShapeDtypeStruct((n, D_MODEL), F32),
                   jax.ShapeDtypeStruct((n, D_FF), F32)],
        scratch_shapes=[pltpu.VMEM((n, D_FF), BF16)],
        compiler_params=_cparams(("arbitrary",)),
        name="convffn_sample",
    )(x, oa, ob, *weights, p1, p2)


def kernel(x_prompt, x_sample, cache_k, cache_v, state_hgrn, state_ffn_conv, page_table,
           ln_in_g, ln_in_b, w_in, hgrn_lb_logits, hgrn_norm_w,
           lambda_q1, lambda_k1, lambda_q2, lambda_k2, diff_norm_w, w_o,
           ln1_g, ln1_b, w_up, conv_w, conv_b, w_down, ln2_g, ln2_b):
    bp, tp_, _ = x_prompt.shape
    bs, ts, _ = x_sample.shape
    pad_t = SAMPLE_PAD_T
    n_phys, page = cache_k.shape[1], cache_k.shape[2]
    slab = page * H_B
    np_rows, ns_rows = bp * tp_, bs * pad_t

    w_in16 = w_in.astype(BF16)
    w_o16 = w_o.astype(BF16)
    w_up16 = w_up.astype(BF16)
    w_dn16 = w_down.astype(BF16)
    lam_vecs = jnp.stack([lambda_q1, lambda_k1, lambda_q2, lambda_k2], axis=1)
    ck = cache_k.reshape(DEPTH, n_phys, slab, 2 * DH_B)
    cv = cache_v.reshape(DEPTH, n_phys, slab, DV_B)
    s_zero = jnp.zeros((1, bp, H_A, DK_A, DV_A), F32)
    per_layer = lambda a: a.reshape(DEPTH, 1, a.shape[-1])
    hgrn_norm_w, diff_norm_w = per_layer(hgrn_norm_w), per_layer(diff_norm_w)
    ffn_weights = (w_o16, per_layer(ln1_g), per_layer(ln1_b), w_up16, conv_w, per_layer(conv_b),
                   w_dn16, per_layer(ln2_g), per_layer(ln2_b))

    xp = _ln_in(x_prompt.reshape(np_rows, D_MODEL), ln_in_g, ln_in_b)
    xs = _ln_in(jnp.pad(x_sample, ((0, 0), (0, pad_t - ts), (0, 0))).reshape(ns_rows, D_MODEL),
                ln_in_g, ln_in_b)

    k_s, v_s, s_p, s_s, c_p, c_s = [], [], [], [], [], []
    kv_flat = [jnp.zeros((DEPTH, np_rows * H_B, DV_B), F32) for _ in range(2)]
    for l in range(DEPTH):
        lam_init = 0.8 - 0.6 * math.exp(-0.3 * l)

        qa, logf, kc, ia, sg, qb, kb16, vb16, *kv_flat = _inproj(
            xp, w_in16, hgrn_lb_logits, l, kv_flat)
        seq = lambda a: a.reshape(bp, tp_, a.shape[-1])
        oa, sp = _hgrn(seq(qa), seq(logf), seq(kc), seq(ia), seq(sg), s_zero, 0,
                       hgrn_norm_w, l, tp_)
        ob = _attn_prompt(seq(qb), seq(kb16), seq(vb16), lam_vecs, diff_norm_w, l, lam_init)
        x2, cp = _ffn_prompt(seq(xp), oa, ob, ffn_weights, l)
        xp = x2.reshape(np_rows, D_MODEL)
        s_p.append(sp)
        c_p.append(cp)

        qa, logf, kc, ia, sg, qb, kb, vb = _inproj(xs, w_in16, hgrn_lb_logits, l)
        seq = lambda a: a.reshape(bs, pad_t, a.shape[-1])
        oa, ss = _hgrn(seq(qa), seq(logf), seq(kc), seq(ia), seq(sg), state_hgrn, l,
                       hgrn_norm_w, l, ts)
        grow = lambda a: jnp.pad(a.reshape(bs, pad_t * H_B, DV_B),
                                 ((0, 0), (0, DEC_NEW_ROWS - pad_t * H_B), (0, 0)))
        ob = _attn_sample(seq(qb), grow(kb), grow(vb), ck, cv, page_table, lam_vecs,
                          diff_norm_w, l, lam_init, ts)
        conv0 = state_ffn_conv[l]
        p1 = jnp.pad(conv0[:, 1:2], ((0, 0), (0, pad_t - 1), (0, 0))).reshape(ns_rows, D_FF)
        p2 = jnp.pad(conv0, ((0, 0), (0, pad_t - 2), (0, 0))).reshape(ns_rows, D_FF)
        xs, a_s = _ffn_sample(xs, oa.reshape(ns_rows, W_A), ob.reshape(ns_rows, W_B),
                              ffn_weights, p1, p2, l, pad_t)
        k_s.append(seq(kb)[:, :ts].reshape(bs, ts, H_B, 2 * DH_B))
        v_s.append(seq(vb)[:, :ts].reshape(bs, ts, H_B, DV_B))
        s_s.append(ss)
        c_s.append(a_s.reshape(bs, pad_t, D_FF)[:, ts - (CONV_W - 1):ts])

    y_p = xp.reshape(bp, tp_, D_MODEL)
    y_s = xs.reshape(bs, pad_t, D_MODEL)[:, :ts]
    k_p, v_p = (a.reshape(DEPTH, bp, tp_, H_B, DV_B) for a in kv_flat)
    return (y_p, y_s, k_p, v_p, jnp.stack(k_s), jnp.stack(v_s),
            jnp.stack(s_p), jnp.stack(s_s), jnp.stack(c_p), jnp.stack(c_s))
```

```python
import functools
import math

import jax
import jax.numpy as jnp
from jax import lax
from jax.experimental import pallas as pl
from jax.experimental.pallas import tpu as pltpu

F32 = jnp.float32
BF16 = jnp.bfloat16

D_MODEL = 1024
DEPTH = 4
H_A, DK_A, DV_A = 4, 128, 128
H_B, DH_B, DV_B = 4, 64, 128
W_A = H_A * DV_A
W_B = H_B * DV_B
IN_COLS = 2 * H_A * DK_A + 2 * W_A + 4 * H_B * DH_B + W_B
D_FF = 2816
FF_CHUNK = 256
FF_ROWS = 128
CONV_W = 3
CHUNK = 128
HGRN_SEQS_PER_STEP = 8
ALPHA = (2 * DEPTH) ** 0.25
LN_EPS = 1e-5
RMS_EPS = 1e-6
LOG2_E = 1.0 / math.log(2.0)
NEG = -1e30
LANES = 128
SAMPLE_PAD_T = 8
DEC_PAGES_PER_CHUNK = 8
DEC_SLOTS = 4
DEC_NEW_ROWS = 128
VMEM_LIMIT = 56 * 1024 * 1024


def _cparams(sem):
    return pltpu.CompilerParams(dimension_semantics=sem, vmem_limit_bytes=VMEM_LIMIT)


def _layernorm(x, g, b):
    mu = jnp.mean(x, axis=-1, keepdims=True)
    xc = x - mu
    var = jnp.mean(xc * xc, axis=-1, keepdims=True)
    return xc * lax.rsqrt(var + LN_EPS) * g + b


def _silu(x):
    return x / (1.0 + jnp.exp2(x * (-LOG2_E)))


def _dot(a, b):
    return jnp.dot(a, b, preferred_element_type=F32)


def _dot_nt(a, b):
    return lax.dot_general(a, b, (((1,), (1,)), ((), ())), preferred_element_type=F32)


def _dot_tn(a, b):
    return lax.dot_general(a, b, (((0,), (0,)), ((), ())), preferred_element_type=F32)


def _lambda(lam_ref, lam_init):
    lv = lam_ref[...]
    p1 = jnp.sum(lv[0:1] * lv[1:2], axis=-1, keepdims=True)
    p2 = jnp.sum(lv[2:3] * lv[3:4], axis=-1, keepdims=True)
    return jnp.exp(p1) - jnp.exp(p2) + lam_init


def _diff_norm(acc1, l1, acc2, l2, lam, nw, lam_init):
    o = acc1 / l1 - lam * (acc2 / l2)
    ms = jnp.mean(o * o, axis=-1, keepdims=True)
    return o * lax.rsqrt(ms + RMS_EPS) * nw * (1.0 - lam_init)


def _ln_kernel(x_ref, g_ref, b_ref, o_ref):
    o_ref[...] = _layernorm(x_ref[...], g_ref[...], b_ref[...])


def _ln_in(x, g, b):
    n = x.shape[0]
    tm = min(512, n)
    return pl.pallas_call(
        _ln_kernel,
        grid=(n // tm,),
        in_specs=[pl.BlockSpec((tm, D_MODEL), lambda i: (i, 0)),
                  pl.BlockSpec((1, D_MODEL), lambda i: (0, 0)),
                  pl.BlockSpec((1, D_MODEL), lambda i: (0, 0))],
        out_specs=pl.BlockSpec((tm, D_MODEL), lambda i: (i, 0)),
        out_shape=jax.ShapeDtypeStruct((n, D_MODEL), F32),
        compiler_params=_cparams(("parallel",)),
        name="ln_in",
    )(x, g.reshape(1, D_MODEL), b.reshape(1, D_MODEL))


def _inproj_kernel(*refs, layer, flat_kv):
    if flat_kv:
        x_ref, w_ref, lbl_ref, _, _ = refs[:5]
        (qa_ref, logf_ref, kc_ref, ia_ref, sg_ref, qb_ref,
         kb16_ref, vb16_ref, kflat_ref, vflat_ref) = refs[-10:]
    else:
        (x_ref, w_ref, lbl_ref, qa_ref, logf_ref, kc_ref, ia_ref, sg_ref,
         qb_ref, kb_ref, vb_ref) = refs
    x = x_ref[...].astype(BF16)
    tm = x.shape[0]

    def proj(c0, n):
        return _dot(x, w_ref[:, c0:c0 + n])

    qa = proj(0, W_A)
    qa_ref[...] = (_silu(qa) * (DK_A ** -0.5)).astype(BF16)

    fa = proj(W_A, W_A)
    log_sig = jnp.minimum(fa, 0.0) - jnp.log1p(jnp.exp(-jnp.abs(fa)))
    sig_neg = 1.0 / (1.0 + jnp.exp(fa))
    if layer == 0:
        logf_ref[...] = log_sig
        kc_ref[...] = sig_neg.astype(BF16)
    else:
        lg = lbl_ref[...]
        e = jnp.exp(lg - jnp.max(lg, axis=0, keepdims=True))
        sm = e / jnp.sum(e, axis=0, keepdims=True)
        lb = sm[1:2]
        for j in range(2, layer + 1):
            lb = lb + sm[j:j + 1]
        u = jnp.log(lb)
        w = jnp.log1p(-lb) + log_sig
        logf_ref[...] = jnp.maximum(u, w) + jnp.log1p(jnp.exp(-jnp.abs(u - w)))
        kc_ref[...] = ((1.0 - lb) * sig_neg).astype(BF16)

    ia_ref[...] = proj(2 * W_A, W_A).astype(BF16)
    sg_ref[...] = _silu(proj(3 * W_A, W_A)).astype(BF16)
    q_scale = DH_B ** -0.5 * (LOG2_E if flat_kv else 1.0)
    qb_ref[...] = (proj(4 * W_A, W_B) * q_scale).astype(qb_ref.dtype)
    kb = proj(4 * W_A + W_B, W_B)
    vb = proj(4 * W_A + 2 * W_B, W_B)
    if flat_kv:
        kb16_ref[...] = kb.astype(BF16)
        vb16_ref[...] = vb.astype(BF16)
        for h in range(H_B):
            hs = slice(h * DV_B, (h + 1) * DV_B)
            kflat_ref[pl.ds(h, tm, stride=H_B), :] = kb[:, hs]
            vflat_ref[pl.ds(h, tm, stride=H_B), :] = vb[:, hs]
    else:
        kb_ref[...] = kb
        vb_ref[...] = vb


def _inproj(x, w_in16, lb_logits, layer, kv_flat=None):
    n = x.shape[0]
    tm = min(512, n)
    row = lambda i: (i, 0)
    f32_out = jax.ShapeDtypeStruct((n, W_A), F32)
    bf_out = jax.ShapeDtypeStruct((n, W_A), BF16)
    blk = pl.BlockSpec((tm, W_A), row)
    in_specs = [pl.BlockSpec((tm, D_MODEL), row),
                pl.BlockSpec((None, D_MODEL, IN_COLS), lambda i: (layer, 0, 0),
                             pipeline_mode=pl.Buffered(1)),
                pl.BlockSpec((DEPTH, W_A), lambda i: (0, 0))]
    if kv_flat is None:
        out_specs = [blk] * 8
        out_shape = [bf_out, f32_out, bf_out, bf_out, bf_out] + [f32_out] * 3
        args, aliases = (), {}
    else:
        flat = jax.ShapeDtypeStruct((DEPTH, n * H_B, DV_B), F32)
        out_specs = [blk] * 8 + [pl.BlockSpec((None, tm * H_B, DV_B), lambda i: (layer, i, 0))] * 2
        out_shape = [bf_out, f32_out] + [bf_out] * 6 + [flat, flat]
        args = tuple(kv_flat)
        in_specs += [pl.BlockSpec(memory_space=pl.ANY)] * 2
        aliases = {3: 8, 4: 9}
    return pl.pallas_call(
        functools.partial(_inproj_kernel, layer=layer, flat_kv=kv_flat is not None),
        grid=(n // tm,),
        in_specs=in_specs,
        out_specs=out_specs,
        out_shape=out_shape,
        input_output_aliases=aliases,
        compiler_params=_cparams(("parallel",)),
        name="inproj",
    )(x, w_in16, lb_logits, *args)


def _boundary_rows(b, m, rows):
    c, w = b.shape
    if 2 * m >= 8:
        parts = [jnp.broadcast_to(b[p * 2 * m + m - 1:p * 2 * m + m], (2 * m, w))
                 for p in range(c // (2 * m))]
        return parts[0] if len(parts) == 1 else jnp.concatenate(parts, axis=0)
    if m == 2:
        r = rows % 4
        return jnp.where(r == 0, pltpu.roll(b, c - 1, 0),
                         jnp.where(r == 1, b,
                                   jnp.where(r == 2, pltpu.roll(b, 1, 0), pltpu.roll(b, 2, 0))))
    return jnp.where(rows % 2 == 0, b, pltpu.roll(b, 1, 0))


def _hgrn_kernel(q_ref, g_ref, kc_ref, v_ref, sg_ref, s0_ref, nw_ref, o_ref, s_ref, st_scr,
                 *, chunk, t_valid, t_total):
    ci = pl.program_id(1)
    n_seq = q_ref.shape[0]

    @pl.when(ci == 0)
    def _():
        for sq in range(n_seq):
            st_scr[sq] = jnp.concatenate([s0_ref[sq, h].T for h in range(H_A)], axis=1)

    for sq in range(n_seq):
        _hgrn_chunk(sq, ci, q_ref, g_ref, kc_ref, v_ref, sg_ref, nw_ref, o_ref, st_scr,
                    chunk=chunk, t_valid=t_valid, t_total=t_total)

    @pl.when(ci == pl.num_programs(1) - 1)
    def _():
        for sq in range(n_seq):
            for h in range(H_A):
                s_ref[sq, h] = st_scr[sq, :, h * DK_A:(h + 1) * DK_A].T


def _hgrn_chunk(sq, ci, q_ref, g_ref, kc_ref, v_ref, sg_ref, nw_ref, o_ref, st_scr,
                *, chunk, t_valid, t_total):
    rows = lax.broadcasted_iota(jnp.int32, (chunk, 1), 0)
    t_idx = lax.broadcasted_iota(jnp.int32, (chunk, chunk), 0)
    s_idx = lax.broadcasted_iota(jnp.int32, (chunk, chunk), 1)
    tri = (s_idx <= t_idx).astype(F32)

    q = q_ref[sq].astype(F32)
    g = g_ref[sq]
    kc = kc_ref[sq].astype(F32)
    if t_valid < t_total:
        valid = (ci * chunk + rows) < t_valid
        g = jnp.where(valid, g, 0.0)
        kc = jnp.where(valid, kc, 0.0)
    v16 = v_ref[sq]
    heads = [slice(h * DK_A, (h + 1) * DK_A) for h in range(H_A)]

    b = jnp.dot(tri, g, preferred_element_type=F32, precision=lax.Precision.HIGHEST) * LOG2_E

    level = jnp.where(s_idx <= t_idx, 31 - lax.clz(t_idx ^ s_idx), -2)
    q16 = q.astype(BF16)
    k16 = kc.astype(BF16)
    a = [jnp.where(level == -1, _dot_nt(q16[:, hs], k16[:, hs]), 0.0) for hs in heads]
    m = chunk // 2
    while m >= 1:
        r = _boundary_rows(b, m, rows)
        second_half = (rows // m) % 2 == 1
        x = (jnp.where(second_half, q, kc) * jnp.exp2(-jnp.abs(b - r))).astype(BF16)
        keep = level == (m.bit_length() - 1)
        a = [jnp.where(keep, _dot_nt(x[:, hs], x[:, hs]), a[h]) for h, hs in enumerate(heads)]
        m //= 2

    qe = (q * jnp.exp2(b)).astype(BF16)
    b_last = b[chunk - 1:chunk]
    decay = jnp.exp2(b_last)
    k_last = (kc * jnp.exp2(b_last - b)).astype(BF16)
    st = st_scr[sq]
    st16 = st.astype(BF16)
    o = [_dot_nt(qe[:, hs], st16[:, hs]) + _dot(a[h].astype(BF16), v16[:, hs])
         for h, hs in enumerate(heads)]
    st_new = st * decay + jnp.concatenate(
        [_dot_tn(v16[:, hs], k_last[:, hs]) for hs in heads], axis=1)
    st_scr[sq] = st_new
    nw = nw_ref[...]
    scale = jnp.concatenate(
        [jnp.broadcast_to(lax.rsqrt(jnp.mean(oh * oh, axis=-1, keepdims=True) + RMS_EPS),
                          (chunk, DV_A)) * nw for oh in o], axis=1)
    o_ref[sq] = (jnp.concatenate(o, axis=1) * scale * sg_ref[sq].astype(F32)).astype(o_ref.dtype)


def _hgrn(qa, logf, kc, ia, sg, s0, s0_layer, norm_w, layer, t_valid):
    bsz, t, _ = qa.shape
    chunk = CHUNK if t % CHUNK == 0 else t
    n_seq = math.gcd(bsz, HGRN_SEQS_PER_STEP if chunk == CHUNK else 2 * HGRN_SEQS_PER_STEP)
    seq = pl.BlockSpec((n_seq, chunk, W_A), lambda b, c: (b, c, 0))
    state = pl.BlockSpec((n_seq, H_A, DK_A, DV_A), lambda b, c: (b, 0, 0, 0))
    return pl.pallas_call(
        functools.partial(_hgrn_kernel, chunk=chunk, t_valid=t_valid, t_total=t),
        grid=(bsz // n_seq, t // chunk),
        in_specs=[seq, seq, seq, seq, seq,
                  pl.BlockSpec((None, n_seq, H_A, DK_A, DV_A), lambda b, c: (s0_layer, b, 0, 0, 0)),
                  pl.BlockSpec((None, 1, DV_A), lambda b, c: (layer, 0, 0))],
        out_specs=[seq, state],
        out_shape=[jax.ShapeDtypeStruct((bsz, t, W_A), BF16),
                   jax.ShapeDtypeStruct((bsz, H_A, DK_A, DV_A), F32)],
        scratch_shapes=[pltpu.VMEM((n_seq, DV_A, H_A * DK_A), F32)],
        compiler_params=_cparams(("parallel", "arbitrary")),
        name="hgrn2",
    )(qa, logf, kc, ia, sg, s0, norm_w)


def _head_slope(h):
    return jnp.where(h == 0, 2.0 ** -2, jnp.where(h == 1, 2.0 ** -4,
                     jnp.where(h == 2, 2.0 ** -6, 2.0 ** -8))).astype(F32)


def _fold_lanes(x, op):
    out = x[:, 0:LANES]
    for c in range(LANES, x.shape[1], LANES):
        out = op(out, x[:, c:c + LANES])
    return out


def _attn_kernel(q_ref, k_ref, v_ref, lam_ref, nw_ref, o_ref, *, tq, n_q, lam_init):
    h = pl.program_id(1)
    slope = _head_slope(h) * LOG2_E
    lam = _lambda(lam_ref, lam_init)
    nw = nw_ref[...]
    lane = lax.broadcasted_iota(jnp.int32, (tq, 2 * DH_B), 1)
    k_diag = lax.broadcasted_iota(jnp.int32, (1, tq), 1)
    causal = k_diag <= lax.broadcasted_iota(jnp.int32, (2 * tq, 1), 0) % tq
    for i in range(n_q):
        qf = q_ref[i * tq:(i + 1) * tq, :].astype(F32)
        q2 = jnp.concatenate([jnp.where(lane < DH_B, qf, 0.0),
                              jnp.where(lane >= DH_B, qf, 0.0)], axis=0).astype(BF16)
        s_d = _dot_nt(q2, k_ref[i * tq:(i + 1) * tq, :]) + slope * k_diag.astype(F32)
        s_d = jnp.where(causal, s_d, NEG)
        m = _fold_lanes(s_d, jnp.maximum)
        if i > 0:
            k_off = lax.broadcasted_iota(jnp.int32, (1, i * tq), 1) - i * tq
            s_o = _dot_nt(q2, k_ref[0:i * tq, :]) + slope * k_off.astype(F32)
            m = jnp.maximum(m, _fold_lanes(s_o, jnp.maximum))
        m = jnp.max(m, axis=-1, keepdims=True)
        p_d = jnp.exp2(s_d - m)
        l = _fold_lanes(p_d, jnp.add)
        acc = _dot(p_d.astype(BF16), v_ref[i * tq:(i + 1) * tq, :])
        if i > 0:
            p_o = jnp.exp2(s_o - m)
            l = l + _fold_lanes(p_o, jnp.add)
            acc = acc + _dot(p_o.astype(BF16), v_ref[0:i * tq, :])
        l = jnp.sum(l, axis=-1, keepdims=True)
        o = _diff_norm(acc[0:tq], l[0:tq], acc[tq:2 * tq], l[tq:2 * tq], lam, nw, lam_init)
        o_ref[i * tq:(i + 1) * tq, :] = o.astype(o_ref.dtype)


def _attn_prompt(qb16, kb16, vb16, lam_vecs, norm_w, layer, lam_init):
    bsz, t, _ = qb16.shape
    tq = min(256, t)
    seq_head = pl.BlockSpec((None, t, DV_B), lambda b, h: (b, 0, h))
    return pl.pallas_call(
        functools.partial(_attn_kernel, tq=tq, n_q=t // tq, lam_init=lam_init),
        grid=(bsz, H_B),
        in_specs=[seq_head, seq_head, seq_head,
                  pl.BlockSpec((None, 4, DH_B), lambda b, h: (layer, 0, 0)),
                  pl.BlockSpec((None, 1, DV_B), lambda b, h: (layer, 0, 0))],
        out_specs=seq_head,
        out_shape=jax.ShapeDtypeStruct((bsz, t, W_B), BF16),
        compiler_params=_cparams(("parallel", "parallel")),
        name="diff_attn_prompt",
    )(qb16, kb16, vb16, lam_vecs, norm_w)


def _softmax_update(s, v16, m_scr, l_scr, acc_scr):
    m_prev = m_scr[...]
    m_new = jnp.maximum(m_prev, jnp.max(s, axis=-1, keepdims=True))
    alpha = jnp.exp(m_prev - m_new)
    p = jnp.exp(s - m_new)
    l_scr[...] = alpha * l_scr[...] + jnp.sum(p, axis=-1, keepdims=True)
    acc_scr[...] = alpha * acc_scr[...] + _dot(p.astype(BF16), v16)
    m_scr[...] = m_new


def _dec_kernel(pt_ref, q_ref, kn_ref, vn_ref, lam_ref, nw_ref, ck_hbm, cv_hbm, o_ref,
                kbuf, vbuf, sem, qx_scr, m_scr, l_scr, acc_scr,
                *, layer, n_pg, n_slots, chunks_per_seq, page, past_len, lam_init, t_valid):
    tp = SAMPLE_PAD_T
    slab = page * H_B
    n_rows = H_B * 2 * tp
    n_chunks = q_ref.shape[0] * chunks_per_seq
    row = lax.broadcasted_iota(jnp.int32, (n_rows, 1), 0)
    row_head = row // (2 * tp)
    slope = jnp.where(row_head == 0, 2.0 ** -2, jnp.where(row_head == 1, 2.0 ** -4,
                      jnp.where(row_head == 2, 2.0 ** -6, 2.0 ** -8))).astype(F32)
    col = lax.broadcasted_iota(jnp.int32, (1, slab), 1)
    own_head = col % H_B == row_head
    lam = _lambda(lam_ref, lam_init)
    nw = nw_ref[...]

    def page_copies(c, slot):
        b = c // chunks_per_seq
        first = (c % chunks_per_seq) * n_pg
        out = []
        for p in range(n_pg):
            pid = pt_ref[b, first + p]
            out.append(pltpu.make_async_copy(ck_hbm.at[layer, pid], kbuf.at[slot, p], sem.at[0, slot, p]))
            out.append(pltpu.make_async_copy(cv_hbm.at[layer, pid], vbuf.at[slot, p], sem.at[1, slot, p]))
        return out

    def start(c, slot):
        for cp in page_copies(c, slot):
            cp.start()

    def begin_sequence(b):
        qf = q_ref[b]
        lane = lax.broadcasted_iota(jnp.int32, (tp, 2 * DH_B), 1)
        parts = []
        for h in range(H_B):
            qh = qf[:, h * 2 * DH_B:(h + 1) * 2 * DH_B]
            parts.append(jnp.where(lane < DH_B, qh, 0.0))
            parts.append(jnp.where(lane >= DH_B, qh, 0.0))
        qx_scr[...] = jnp.concatenate(parts, axis=0).astype(BF16)
        m_scr[...] = jnp.full(m_scr.shape, NEG, F32)
        l_scr[...] = jnp.zeros(l_scr.shape, F32)
        acc_scr[...] = jnp.zeros(acc_scr.shape, F32)

    def end_sequence(b):
        col_n = lax.broadcasted_iota(jnp.int32, (1, kn_ref.shape[1]), 1)
        t_k = col_n // H_B
        s_n = _dot_nt(qx_scr[...], kn_ref[b].astype(BF16))
        ok = (col_n % H_B == row_head) & (t_k <= row % tp) & (t_k < t_valid)
        s_n = jnp.where(ok, s_n + slope * t_k.astype(F32), NEG)
        _softmax_update(s_n, vn_ref[b].astype(BF16), m_scr, l_scr, acc_scr)
        outs = []
        for h in range(H_B):
            r0 = h * 2 * tp
            outs.append(_diff_norm(acc_scr[r0:r0 + tp], l_scr[r0:r0 + tp],
                                   acc_scr[r0 + tp:r0 + 2 * tp], l_scr[r0 + tp:r0 + 2 * tp],
                                   lam, nw, lam_init))
        o_ref[b] = jnp.concatenate(outs, axis=1).astype(o_ref.dtype)

    for c in range(n_slots - 1):
        start(c, c)

    def group(it, carry):
        for u in range(n_slots):
            c = it * n_slots + u
            b = c // chunks_per_seq
            jc = c % chunks_per_seq

            @pl.when(c + n_slots - 1 < n_chunks)
            def _():
                start(c + n_slots - 1, (u + n_slots - 1) % n_slots)

            if u == 0:
                pl.when(jc == 0)(functools.partial(begin_sequence, b))

            for cp in page_copies(c, u):
                cp.wait()

            qx = qx_scr[...]
            pos0 = col // H_B + (jc * (n_pg * page) - past_len)
            s = [jnp.where(own_head,
                           _dot_nt(qx, kbuf[u, p].astype(BF16))
                           + slope * (pos0 + p * page).astype(F32), NEG) for p in range(n_pg)]
            s_max = s[0]
            for sp in s[1:]:
                s_max = jnp.maximum(s_max, sp)
            m_prev = m_scr[...]
            m_new = jnp.maximum(m_prev, jnp.max(s_max, axis=-1, keepdims=True))
            alpha = jnp.exp(m_prev - m_new)
            pr = [jnp.exp(sp - m_new) for sp in s]
            p_sum = pr[0]
            for pp in pr[1:]:
                p_sum = p_sum + pp
            pv = _dot(pr[0].astype(BF16), vbuf[u, 0].astype(BF16))
            for p in range(1, n_pg):
                pv = pv + _dot(pr[p].astype(BF16), vbuf[u, p].astype(BF16))
            l_scr[...] = alpha * l_scr[...] + jnp.sum(p_sum, axis=-1, keepdims=True)
            acc_scr[...] = alpha * acc_scr[...] + pv
            m_scr[...] = m_new

            if u == n_slots - 1:
                pl.when(jc == chunks_per_seq - 1)(functools.partial(end_sequence, b))
        return carry

    lax.fori_loop(0, n_chunks // n_slots, group, 0)


def _attn_sample(qb, kn_pad, vn_pad, cache_k, cache_v, page_table, lam_vecs, norm_w,
                 layer, lam_init, t_valid):
    bsz, tp, _ = qb.shape
    n_pages = page_table.shape[1]
    slab = cache_k.shape[2]
    page = slab // H_B
    n_pg, n_slots = DEC_PAGES_PER_CHUNK, DEC_SLOTS
    chunks_per_seq = n_pages // n_pg
    assert n_pages % n_pg == 0 and chunks_per_seq % n_slots == 0, (n_pages, n_pg, n_slots)
    n_rows = H_B * 2 * tp
    n_new = kn_pad.shape[1]
    whole = lambda *shape: pl.BlockSpec(shape, lambda i, pt: (0,) * len(shape))
    grid_spec = pltpu.PrefetchScalarGridSpec(
        num_scalar_prefetch=1,
        grid=(1,),
        in_specs=[whole(bsz, tp, W_B), whole(bsz, n_new, DV_B), whole(bsz, n_new, DV_B),
                  pl.BlockSpec((None, 4, DH_B), lambda i, pt: (layer, 0, 0)),
                  pl.BlockSpec((None, 1, DV_B), lambda i, pt: (layer, 0, 0)),
                  pl.BlockSpec(memory_space=pl.ANY), pl.BlockSpec(memory_space=pl.ANY)],
        out_specs=whole(bsz, tp, W_B),
        scratch_shapes=[pltpu.VMEM((n_slots, n_pg, slab, DV_B), F32),
                        pltpu.VMEM((n_slots, n_pg, slab, DV_B), F32),
                        pltpu.SemaphoreType.DMA((2, n_slots, n_pg)),
                        pltpu.VMEM((n_rows, 2 * DH_B), BF16),
                        pltpu.VMEM((n_rows, 1), F32), pltpu.VMEM((n_rows, 1), F32),
                        pltpu.VMEM((n_rows, DV_B), F32)],
    )
    return pl.pallas_call(
        functools.partial(_dec_kernel, layer=layer, n_pg=n_pg, n_slots=n_slots,
                          chunks_per_seq=chunks_per_seq, page=page, past_len=n_pages * page,
                          lam_init=lam_init, t_valid=t_valid),
        grid_spec=grid_spec,
        out_shape=jax.ShapeDtypeStruct((bsz, tp, W_B), F32),
        compiler_params=_cparams(("arbitrary",)),
        name="diff_attn_sample",
    )(page_table, qb, kn_pad, vn_pad, lam_vecs, norm_w, cache_k, cache_v)


def _ffn_kernel(*refs, tm, seq_len, per_row_state):
    (x_ref, oa_ref, ob_ref, wo_ref, g1_ref, b1_ref,
     wup_ref, cw_ref, cb_ref, wdn_ref, g_ref, b_ref) = refs[:12]
    if per_row_state:
        p1_ref, p2_ref, o_ref, a_ref, h_scr = refs[12:]
    else:
        o_ref, cs_ref, h_scr, a_scr, g_scr = refs[12:]
        ti = pl.program_id(1)

        @pl.when(ti == 0)
        def _():
            a_scr[0:8, :] = jnp.zeros((8, D_FF), F32)

    mixed = (_dot(oa_ref[...].astype(BF16), wo_ref[0:W_A, :])
             + _dot(ob_ref[...].astype(BF16), wo_ref[W_A:W_A + W_B, :]))
    x = _layernorm(ALPHA * x_ref[...] + mixed, g1_ref[...], b1_ref[...])
    x16 = x.astype(BF16)
    for c0 in range(0, D_FF, FF_CHUNK):
        cs = slice(c0, c0 + FF_CHUNK)
        a = _dot(x16, wup_ref[:, cs])
        g = _dot(x16, wup_ref[:, D_FF + c0:D_FF + c0 + FF_CHUNK])
        cw = cw_ref[:, cs]
        cb = cb_ref[:, cs]
        if per_row_state:
            t = lax.broadcasted_iota(jnp.int32, (tm, 1), 0) % seq_len
            am1 = jnp.where(t >= 1, pltpu.roll(a, 1, 0), p1_ref[:, cs])
            am2 = jnp.where(t >= 2, pltpu.roll(a, 2, 0), p2_ref[:, cs])
            a_ref[:, cs] = a
            c = cb + am2 * cw[0:1] + am1 * cw[1:2] + a * cw[2:3]
            h_scr[:, cs] = (_silu(c) * g).astype(BF16)
        else:
            a_scr[8:8 + tm, cs] = a
            g_scr[:, cs] = g
            for r0 in range(0, tm, FF_ROWS):
                c = (cb + a_scr[6 + r0:6 + r0 + FF_ROWS, cs] * cw[0:1]
                     + a_scr[7 + r0:7 + r0 + FF_ROWS, cs] * cw[1:2]
                     + a_scr[8 + r0:8 + r0 + FF_ROWS, cs] * cw[2:3])
                h_scr[r0:r0 + FF_ROWS, cs] = (
                    _silu(c) * g_scr[r0:r0 + FF_ROWS, cs]).astype(BF16)
    if not per_row_state:
        @pl.when(ti == pl.num_programs(1) - 1)
        def _():
            cs_ref[...] = a_scr[6 + tm:8 + tm, :]

        a_scr[0:8, :] = a_scr[tm:tm + 8, :]
    y = _dot(h_scr[...], wdn_ref[...])
    o_ref[...] = _layernorm(ALPHA * x + y, g_ref[...], b_ref[...])


def _ffn_specs(layer):
    once = pl.Buffered(1)
    return [pl.BlockSpec((None, D_MODEL, D_MODEL), lambda *a: (layer, 0, 0), pipeline_mode=once),
            pl.BlockSpec((None, 1, D_MODEL), lambda *a: (layer, 0, 0)),
            pl.BlockSpec((None, 1, D_MODEL), lambda *a: (layer, 0, 0)),
            pl.BlockSpec((None, D_MODEL, 2 * D_FF), lambda *a: (layer, 0, 0), pipeline_mode=once),
            pl.BlockSpec((None, CONV_W, D_FF), lambda *a: (layer, 0, 0)),
            pl.BlockSpec((None, 1, D_FF), lambda *a: (layer, 0, 0)),
            pl.BlockSpec((None, D_FF, D_MODEL), lambda *a: (layer, 0, 0), pipeline_mode=once),
            pl.BlockSpec((None, 1, D_MODEL), lambda *a: (layer, 0, 0)),
            pl.BlockSpec((None, 1, D_MODEL), lambda *a: (layer, 0, 0))]


def _ffn_prompt(x, oa, ob, weights, layer):
    bsz, t, _ = x.shape
    tm = min(512, t)
    tile = lambda w: pl.BlockSpec((None, tm, w), lambda b, i: (b, i, 0))
    return pl.pallas_call(
        functools.partial(_ffn_kernel, tm=tm, seq_len=t, per_row_state=False),
        grid=(bsz, t // tm),
        in_specs=[tile(D_MODEL), tile(W_A), tile(W_B)] + _ffn_specs(layer),
        out_specs=[pl.BlockSpec((None, tm, D_MODEL), lambda b, i: (b, i, 0)),
                   pl.BlockSpec((None, CONV_W - 1, D_FF), lambda b, i: (b, 0, 0))],
        out_shape=[jax.ShapeDtypeStruct((bsz, t, D_MODEL), F32),
                   jax.ShapeDtypeStruct((bsz, CONV_W - 1, D_FF), F32)],
        scratch_shapes=[pltpu.VMEM((tm, D_FF), BF16), pltpu.VMEM((8 + tm, D_FF), F32),
                        pltpu.VMEM((tm, D_FF), F32)],
        compiler_params=_cparams(("parallel", "arbitrary")),
        name="convffn_prompt",
    )(x, oa, ob, *weights)


def _ffn_sample(x, oa, ob, weights, p1, p2, layer, seq_len):
    n = x.shape[0]
    full = lambda i: (0, 0)
    whole = lambda w: pl.BlockSpec((n, w), full)
    return pl.pallas_call(
        functools.partial(_ffn_kernel, tm=n, seq_len=seq_len, per_row_state=True),
        grid=(1,),
        in_specs=[whole(D_MODEL), whole(W_A), whole(W_B)] + _ffn_specs(layer)
                 + [whole(D_FF), whole(D_FF)],
        out_specs=[pl.BlockSpec((n, D_MODEL), full), pl.BlockSpec((n, D_FF), full)],
        out_shape=[jax.ShapeDtypeStruct((n, D_MODEL), F32),
                   jax.ShapeDtypeStruct((n, D_FF), F32)],
        scratch_shapes=[pltpu.VMEM((n, D_FF), BF16)],
        compiler_params=_cparams(("arbitrary",)),
        name="convffn_sample",
    )(x, oa, ob, *weights, p1, p2)


def kernel(x_prompt, x_sample, cache_k, cache_v, state_hgrn, state_ffn_conv, page_table,
           ln_in_g, ln_in_b, w_in, hgrn_lb_logits, hgrn_norm_w,
           lambda_q1, lambda_k1, lambda_q2, lambda_k2, diff_norm_w, w_o,
           ln1_g, ln1_b, w_up, conv_w, conv_b, w_down, ln2_g, ln2_b):
    bp, tp_, _ = x_prompt.shape
    bs, ts, _ = x_sample.shape
    pad_t = SAMPLE_PAD_T
    n_phys, page = cache_k.shape[1], cache_k.shape[2]
    slab = page * H_B
    np_rows, ns_rows = bp * tp_, bs * pad_t

    w_in16 = w_in.astype(BF16)
    w_o16 = w_o.astype(BF16)
    w_up16 = w_up.astype(BF16)
    w_dn16 = w_down.astype(BF16)
    lam_vecs = jnp.stack([lambda_q1, lambda_k1, lambda_q2, lambda_k2], axis=1)
    ck = cache_k.reshape(DEPTH, n_phys, slab, 2 * DH_B)
    cv = cache_v.reshape(DEPTH, n_phys, slab, DV_B)
    s_zero = jnp.zeros((1, bp, H_A, DK_A, DV_A), F32)
    per_layer = lambda a: a.reshape(DEPTH, 1, a.shape[-1])
    hgrn_norm_w, diff_norm_w = per_layer(hgrn_norm_w), per_layer(diff_norm_w)
    ffn_weights = (w_o16, per_layer(ln1_g), per_layer(ln1_b), w_up16, conv_w, per_layer(conv_b),
                   w_dn16, per_layer(ln2_g), per_layer(ln2_b))

    xp = _ln_in(x_prompt.reshape(np_rows, D_MODEL), ln_in_g, ln_in_b)
    xs = _ln_in(jnp.pad(x_sample, ((0, 0), (0, pad_t - ts), (0, 0))).reshape(ns_rows, D_MODEL),
                ln_in_g, ln_in_b)

    k_s, v_s, s_p, s_s, c_p, c_s = [], [], [], [], [], []
    kv_flat = [jnp.zeros((DEPTH, np_rows * H_B, DV_B), F32) for _ in range(2)]
    for l in range(DEPTH):
        lam_init = 0.8 - 0.6 * math.exp(-0.3 * l)

        qa, logf, kc, ia, sg, qb, kb16, vb16, *kv_flat = _inproj(
            xp, w_in16, hgrn_lb_logits, l, kv_flat)
        seq = lambda a: a.reshape(bp, tp_, a.shape[-1])
        oa, sp = _hgrn(seq(qa), seq(logf), seq(kc), seq(ia), seq(sg), s_zero, 0,
                       hgrn_norm_w, l, tp_)
        ob = _attn_prompt(seq(qb), seq(kb16), seq(vb16), lam_vecs, diff_norm_w, l, lam_init)
        x2, cp = _ffn_prompt(seq(xp), oa, ob, ffn_weights, l)
        xp = x2.reshape(np_rows, D_MODEL)
        s_p.append(sp)
        c_p.append(cp)

        qa, logf, kc, ia, sg, qb, kb, vb = _inproj(xs, w_in16, hgrn_lb_logits, l)
        seq = lambda a: a.reshape(bs, pad_t, a.shape[-1])
        oa, ss = _hgrn(seq(qa), seq(logf), seq(kc), seq(ia), seq(sg), state_hgrn, l,
                       hgrn_norm_w, l, ts)
        grow = lambda a: jnp.pad(a.reshape(bs, pad_t * H_B, DV_B),
                                 ((0, 0), (0, DEC_NEW_ROWS - pad_t * H_B), (0, 0)))
        ob = _attn_sample(seq(qb), grow(kb), grow(vb), ck, cv, page_table, lam_vecs,
                          diff_norm_w, l, lam_init, ts)
        conv0 = state_ffn_conv[l]
        p1 = jnp.pad(conv0[:, 1:2], ((0, 0), (0, pad_t - 1), (0, 0))).reshape(ns_rows, D_FF)
        p2 = jnp.pad(conv0, ((0, 0), (0, pad_t - 2), (0, 0))).reshape(ns_rows, D_FF)
        xs, a_s = _ffn_sample(xs, oa.reshape(ns_rows, W_A), ob.reshape(ns_rows, W_B),
                              ffn_weights, p1, p2, l, pad_t)
        k_s.append(seq(kb)[:, :ts].reshape(bs, ts, H_B, 2 * DH_B))
        v_s.append(seq(vb)[:, :ts].reshape(bs, ts, H_B, DV_B))
        s_s.append(ss)
        c_s.append(a_s.reshape(bs, pad_t, D_FF)[:, ts - (CONV_W - 1):ts])

    y_p = xp.reshape(bp, tp_, D_MODEL)
    y_s = xs.reshape(bs, pad_t, D_MODEL)[:, :ts]
    k_p, v_p = (a.reshape(DEPTH, bp, tp_, H_B, DV_B) for a in kv_flat)
    return (y_p, y_s, k_p, v_p, jnp.stack(k_s), jnp.stack(v_s),
            jnp.stack(s_p), jnp.stack(s_s), jnp.stack(c_p), jnp.stack(c_s))
```
